```python
import math
import jax, jax.numpy as jnp
from jax import lax
import numpy as np

D_MODEL = 1024
BATCH = 2
SEQ = 16384
DEPTH = 1
DEC_BATCH = 128
DEC_SEQ = 4
PAST_LEN = 8192
PAGE_SIZE = 128

N_META = 16
RET_HEADS = 4
RET_DK = 128
RET_DV = D_MODEL // RET_HEADS
RET_CHUNK = 128
DIFF_HEADS = 8
DIFF_DH = D_MODEL // (2 * DIFF_HEADS)
DIFF_DV = 2 * DIFF_DH
Q_BLOCK = 128
D_FF = -(-8 * D_MODEL // (3 * 256)) * 256
RMS_EPS = 1e-6

RET_QK_W = RET_HEADS * RET_DK
RET_V_W = RET_HEADS * RET_DV
DIFF_QK_W = DIFF_HEADS * 2 * DIFF_DH
DIFF_V_W = DIFF_HEADS * DIFF_DV
SPLIT_SIZES = (RET_QK_W, RET_QK_W, RET_V_W, RET_V_W, DIFF_QK_W, DIFF_QK_W, DIFF_V_W, D_MODEL, D_MODEL)
IN_WIDTH = sum(SPLIT_SIZES)

kernel_name = "hybrid_retention_diffattn_decode_step"


def rmsnorm(x, w):
    xf = x.astype(jnp.float32)
    y = xf * lax.rsqrt(jnp.mean(xf * xf, axis=-1, keepdims=True) + RMS_EPS)
    return (y * w.astype(jnp.float32)).astype(x.dtype)


def retention_log_decay():
    return jnp.log1p(-jnp.exp2(-5.0 - jnp.arange(RET_HEADS, dtype=jnp.float32)))


def alibi_slopes():
    return jnp.exp2(-8.0 * (jnp.arange(DIFF_HEADS, dtype=jnp.float32) + 1.0) / DIFF_HEADS)


def split_proj(h, w_in):
    b, t = h.shape[:2]
    p = h @ w_in
    idx = np.cumsum(SPLIT_SIZES)[:-1].tolist()
    rq, rk, rv, rg, dq, dk, dv, ga, gb = jnp.split(p, idx, axis=-1)
    rq = rq.reshape(b, t, RET_HEADS, RET_DK)
    rk = rk.reshape(b, t, RET_HEADS, RET_DK) * (RET_DK ** -0.5)
    rv = rv.reshape(b, t, RET_HEADS, RET_DV)
    rg = rg.reshape(b, t, RET_HEADS, RET_DV)
    dq = dq.reshape(b, t, DIFF_HEADS, 2, DIFF_DH)
    dk = dk.reshape(b, t, DIFF_HEADS, 2, DIFF_DH)
    dv = dv.reshape(b, t, DIFF_HEADS, DIFF_DV)
    return rq, rk, rv, rg, dq, dk, dv, ga, gb


def retention_chunk(q, k, v, state, log_g):
    c = q.shape[1]
    dt = q.dtype
    idx = jnp.arange(c, dtype=jnp.float32)
    rel = idx[:, None] - idx[None, :]
    decay_in = jnp.where(rel >= 0, jnp.exp(jnp.maximum(rel, 0.0)[None] * log_g[:, None, None]), 0.0).astype(dt)
    q_dec = jnp.exp((idx + 1.0)[None, :] * log_g[:, None]).astype(dt)
    k_dec = jnp.exp((c - 1.0 - idx)[None, :] * log_g[:, None]).astype(dt)
    s_dec = jnp.exp(c * log_g).astype(dt)
    scores = jnp.einsum('bihd,bjhd->bhij', q, k) * decay_in
    o = jnp.einsum('bhij,bjhe->bihe', scores, v) + jnp.einsum('bihd,hi,bhde->bihe', q, q_dec, state)
    new_state = s_dec[None, :, None, None] * state + jnp.einsum('bjhd,hj,bjhe->bhde', k, k_dec, v)
    return o, new_state


def retention_prompt(q, k, v, log_g):
    b = q.shape[0]
    s0 = jnp.zeros((b, RET_HEADS, RET_DK, RET_DV), q.dtype)
    o_meta, s = retention_chunk(q[:, :N_META], k[:, :N_META], v[:, :N_META], s0, log_g)

    def to_chunks(a):
        rest = a[:, N_META:]
        n = rest.shape[1] // RET_CHUNK
        return rest.reshape(b, n, RET_CHUNK, *a.shape[2:]).swapaxes(0, 1)

    def step(carry, c):
        o, carry = retention_chunk(c[0], c[1], c[2], carry, log_g)
        return carry, o

    s, o = lax.scan(step, s, (to_chunks(q), to_chunks(k), to_chunks(v)))
    o = o.swapaxes(0, 1).reshape(b, -1, RET_HEADS, RET_DV)
    return jnp.concatenate([o_meta, o], axis=1), s


def diff_attend(q, k, v, q_pos, k_pos, lam, slopes):
    s = jnp.einsum('bqhcd,bkhcd->bchqk', q, k).astype(jnp.float32) * (DIFF_DH ** -0.5)
    dist = (q_pos[:, None] - k_pos[None, :]).astype(jnp.float32)
    bias = -slopes[:, None, None] * dist
    s = jnp.where(dist >= 0, s + bias, -jnp.inf)
    p = jax.nn.softmax(s, axis=-1)
    a = (p[:, 0] - lam * p[:, 1]).astype(v.dtype)
    return jnp.einsum('bhqk,bkhd->bqhd', a, v)


def diff_prompt(q, k, v, lam, slopes):
    b, length = q.shape[:2]
    pos = jnp.arange(length, dtype=jnp.int32)
    o_meta = diff_attend(q[:, :N_META], k[:, :N_META], v[:, :N_META], pos[:N_META], pos[:N_META], lam, slopes)
    n = (length - N_META) // Q_BLOCK
    qb = q[:, N_META:].reshape(b, n, Q_BLOCK, DIFF_HEADS, 2, DIFF_DH).swapaxes(0, 1)
    pb = pos[N_META:].reshape(n, Q_BLOCK)
    o = lax.map(lambda a: diff_attend(a[0], k, v, a[1], pos, lam, slopes), (qb, pb))
    o = o.swapaxes(0, 1).reshape(b, length - N_META, DIFF_HEADS, DIFF_DV)
    return jnp.concatenate([o_meta, o], axis=1)


def merge_branches(o_ret, r_gate, o_diff, g_a, g_b, ret_norm_w, diff_norm_w, lam_init, w_out):
    b, t = o_ret.shape[:2]
    a = (rmsnorm(o_ret, ret_norm_w) * jax.nn.silu(r_gate)).reshape(b, t, D_MODEL)
    d = (rmsnorm(o_diff, diff_norm_w) * (1.0 - lam_init)).reshape(b, t, D_MODEL)
    m = jax.nn.sigmoid(g_a) * a + jax.nn.sigmoid(g_b) * d
    return m @ w_out


def swiglu(h, w_gate, w_up, w_down):
    return (jax.nn.silu(h @ w_gate) * (h @ w_up)) @ w_down


def setup_inputs(seed: int = 0) -> dict:
    key = jax.random.key(seed)
    ks = jax.random.split(key, 24)
    f32 = jnp.float32
    n_pages = PAST_LEN // PAGE_SIZE
    n_pool = (DEC_BATCH * n_pages * 5) // 4

    def nrm(k, shape, scale):
        return jax.random.normal(k, shape, f32) * scale

    def gain(k, shape):
        return 1.0 + nrm(k, shape, 0.05)

    page_table = jax.random.permutation(ks[5], n_pool)[: DEC_BATCH * n_pages].reshape(DEC_BATCH, n_pages).astype(jnp.int32)
    return {
        "x_prompt": nrm(ks[0], (BATCH, SEQ, D_MODEL), 1.0),
        "x_sample": nrm(ks[1], (DEC_BATCH, DEC_SEQ, D_MODEL), 1.0),
        "cache_k": nrm(ks[2], (DEPTH, n_pool, PAGE_SIZE, DIFF_HEADS, 2 * DIFF_DH), 1.0),
        "cache_v": nrm(ks[3], (DEPTH, n_pool, PAGE_SIZE, DIFF_HEADS, DIFF_DV), 1.0),
        "state_ret": nrm(ks[4], (DEPTH, DEC_BATCH, RET_HEADS, RET_DK, RET_DV), 0.5),
        "page_table": page_table,
        "meta_tokens": nrm(ks[6], (N_META, D_MODEL), 1.0),
        "norm_mix_pre": gain(ks[7], (DEPTH, D_MODEL)),
        "norm_mix_post": gain(ks[8], (DEPTH, D_MODEL)),
        "w_in": nrm(ks[9], (DEPTH, D_MODEL, IN_WIDTH), D_MODEL ** -0.5),
        "ret_norm": gain(ks[10], (DEPTH, RET_DV)),
        "lambda_q1": nrm(ks[11], (DEPTH, DIFF_DH), 0.1),
        "lambda_k1": nrm(ks[12], (DEPTH, DIFF_DH), 0.1),
        "lambda_q2": nrm(ks[13], (DEPTH, DIFF_DH), 0.1),
        "lambda_k2": nrm(ks[14], (DEPTH, DIFF_DH), 0.1),
        "diff_norm": gain(ks[15], (DEPTH, DIFF_DV)),
        "w_out": nrm(ks[16], (DEPTH, D_MODEL, D_MODEL), D_MODEL ** -0.5),
        "norm_ffn_pre": gain(ks[17], (DEPTH, D_MODEL)),
        "norm_ffn_post": gain(ks[18], (DEPTH, D_MODEL)),
        "w_ffn_gate": nrm(ks[19], (DEPTH, D_MODEL, D_FF), D_MODEL ** -0.5),
        "w_ffn_up": nrm(ks[20], (DEPTH, D_MODEL, D_FF), D_MODEL ** -0.5),
        "w_ffn_down": nrm(ks[21], (DEPTH, D_FF, D_MODEL), D_FF ** -0.5),
    }


def reference(x_prompt, x_sample, cache_k, cache_v, state_ret, page_table, meta_tokens,
              norm_mix_pre, norm_mix_post, w_in, ret_norm, lambda_q1, lambda_k1, lambda_q2,
              lambda_k2, diff_norm, w_out, norm_ffn_pre, norm_ffn_post, w_ffn_gate, w_ffn_up,
              w_ffn_down):
    b = x_prompt.shape[0]
    db, t_new = x_sample.shape[:2]
    past = page_table.shape[1] * PAGE_SIZE
    log_g = retention_log_decay()
    slopes = alibi_slopes()

    xp = jnp.concatenate(
        [jnp.broadcast_to(meta_tokens.astype(x_prompt.dtype)[None], (b, N_META, D_MODEL)), x_prompt], axis=1)
    length = xp.shape[1]
    xs = x_sample
    pos_k_s = jnp.arange(past + t_new, dtype=jnp.int32)
    pos_q_s = past + jnp.arange(t_new, dtype=jnp.int32)

    kp_l, vp_l, sp_l, ks_l, vs_l, ss_l = [], [], [], [], [], []
    for l in range(DEPTH):
        lam_init = 0.8 - 0.6 * math.exp(-0.3 * l)
        lam = (jnp.exp(jnp.sum(lambda_q1[l].astype(jnp.float32) * lambda_k1[l].astype(jnp.float32)))
               - jnp.exp(jnp.sum(lambda_q2[l].astype(jnp.float32) * lambda_k2[l].astype(jnp.float32)))
               + lam_init)

        h = rmsnorm(xp, norm_mix_pre[l])
        rq, rk, rv, rg, dq, dk, dv, ga, gb = split_proj(h, w_in[l])
        o_ret, s_fin = retention_prompt(rq, rk, rv, log_g)
        o_diff = diff_prompt(dq, dk, dv, lam, slopes)
        mix = merge_branches(o_ret, rg, o_diff, ga, gb, ret_norm[l], diff_norm[l], lam_init, w_out[l])
        xp = xp + rmsnorm(mix, norm_mix_post[l])
        xp = xp + rmsnorm(swiglu(rmsnorm(xp, norm_ffn_pre[l]), w_ffn_gate[l], w_ffn_up[l], w_ffn_down[l]), norm_ffn_post[l])
        kp_l.append(dk.reshape(b, length, DIFF_HEADS, 2 * DIFF_DH))
        vp_l.append(dv)
        sp_l.append(s_fin)

        h = rmsnorm(xs, norm_mix_pre[l])
        rq, rk, rv, rg, dq, dk, dv, ga, gb = split_proj(h, w_in[l])
        o_ret_s, s_new = retention_chunk(rq, rk, rv, state_ret[l], log_g)
        k_past = cache_k[l, page_table].reshape(db, past, DIFF_HEADS, 2, DIFF_DH)
        v_past = cache_v[l, page_table].reshape(db, past, DIFF_HEADS, DIFF_DV)
        k_all = jnp.concatenate([k_past, dk], axis=1)
        v_all = jnp.concatenate([v_past, dv], axis=1)
        o_diff_s = diff_attend(dq, k_all, v_all, pos_q_s, pos_k_s, lam, slopes)
        mix = merge_branches(o_ret_s, rg, o_diff_s, ga, gb, ret_norm[l], diff_norm[l], lam_init, w_out[l])
        xs = xs + rmsnorm(mix, norm_mix_post[l])
        xs = xs + rmsnorm(swiglu(rmsnorm(xs, norm_ffn_pre[l]), w_ffn_gate[l], w_ffn_up[l], w_ffn_down[l]), norm_ffn_post[l])
        ks_l.append(dk.reshape(db, t_new, DIFF_HEADS, 2 * DIFF_DH))
        vs_l.append(dv)
        ss_l.append(s_new)

    y_prompt = xp[:, N_META:]
    y_sample = xs
    return (y_prompt, y_sample, jnp.stack(kp_l), jnp.stack(vp_l), jnp.stack(sp_l),
            jnp.stack(ks_l), jnp.stack(vs_l), jnp.stack(ss_l))
```

```python
import functools
import math

import jax
import jax.numpy as jnp
from jax import lax
from jax.experimental import pallas as pl
from jax.experimental.pallas import tpu as pltpu

F32 = jnp.float32
BF16 = jnp.bfloat16

RMS_EPS = 1e-6
N_META = 16
RET_HEADS = 4
RET_DK = 128
DIFF_HEADS = 8
DIFF_DH = 64
HEAD_W = 2 * DIFF_DH
VMEM_LIMIT = 56 * 1024 * 1024
NT_DIMS = (((1,), (1,)), ((), ()))
TN_DIMS = (((0,), (0,)), ((), ()))


def _params(semantics):
    return pltpu.CompilerParams(dimension_semantics=semantics, vmem_limit_bytes=VMEM_LIMIT)


def _resident(shape):
    return pl.BlockSpec(shape, lambda *_: (0,) * len(shape), pipeline_mode=pl.Buffered(1))


def _rms(x, w):
    ms = jnp.mean(x * x, axis=-1, keepdims=True)
    return x * lax.rsqrt(ms + RMS_EPS) * w


def _norm_matmul_kernel(x_ref, nw_ref, w_ref, *refs):
    out_refs, h_ref = refs[:-1], refs[-1]

    @pl.when(pl.program_id(1) == 0)
    def _():
        h_ref[...] = _rms(x_ref[...], nw_ref[...]).astype(BF16)

    acc = jnp.dot(h_ref[...], w_ref[...], preferred_element_type=F32)
    for o in out_refs:
        o[...] = acc.astype(o.dtype)


def norm_matmul(x, nw, w, out_dtypes, tm, tn):
    t, k = x.shape
    n = w.shape[1]
    tm = min(tm, t)
    tn = min(tn, n)
    assert t % tm == 0 and n % tn == 0
    return pl.pallas_call(
        _norm_matmul_kernel,
        grid=(t // tm, n // tn),
        in_specs=[
            pl.BlockSpec((tm, k), lambda i, j: (i, 0)),
            pl.BlockSpec((1, k), lambda i, j: (0, 0)),
            pl.BlockSpec((k, tn), lambda i, j: (0, j)),
        ],
        out_specs=[pl.BlockSpec((tm, tn), lambda i, j: (i, j)) for _ in out_dtypes],
        out_shape=[jax.ShapeDtypeStruct((t, n), d) for d in out_dtypes],
        scratch_shapes=[pltpu.VMEM((tm, k), BF16)],
        compiler_params=_params(("parallel", "arbitrary")),
        name="norm_matmul",
    )(x, nw.reshape(1, k), w)


def _retention_kernel(logg_ref, q_ref, k_ref, v_ref, s0_ref, o_ref, sfin_ref, state_ref, *, n_valid):
    h = pl.program_id(1)
    c = pl.program_id(2)
    chunk = q_ref.shape[1]
    lg = logg_ref[h]

    @pl.when(c == 0)
    def _():
        state_ref[...] = s0_ref[0, 0]

    q = q_ref[0].astype(F32)
    k = k_ref[0].astype(F32) * (RET_DK ** -0.5)
    v = v_ref[0]
    row = lax.broadcasted_iota(jnp.int32, (chunk, chunk), 0)
    col = lax.broadcasted_iota(jnp.int32, (chunk, chunk), 1)
    rel = (row - col).astype(F32)
    decay_in = jnp.where(rel >= 0, jnp.exp(jnp.maximum(rel, 0.0) * lg), 0.0)
    idx = lax.broadcasted_iota(jnp.int32, (chunk, 1), 0).astype(F32)
    q_dec = jnp.exp((idx + 1.0) * lg)
    k_dec = jnp.exp((n_valid - 1.0 - idx) * lg)
    s_dec = jnp.exp(jnp.full((1, 1), float(n_valid), F32) * lg)

    state = state_ref[...]
    scores = lax.dot_general(q.astype(BF16), k.astype(BF16), NT_DIMS, preferred_element_type=F32) * decay_in
    o = jnp.dot(scores.astype(BF16), v, preferred_element_type=F32)
    o = o + jnp.dot((q * q_dec).astype(BF16), state.astype(BF16), preferred_element_type=F32)
    o_ref[0] = o.astype(o_ref.dtype)

    state = s_dec * state + lax.dot_general((k * k_dec).astype(BF16), v, TN_DIMS, preferred_element_type=F32)
    state_ref[...] = state

    @pl.when(c == pl.num_programs(2) - 1)
    def _():
        sfin_ref[0, 0] = state


def retention(log_g, qkv, s0, chunk, n_valid):
    b, t, w = qkv.shape
    dv = (w - 2 * RET_HEADS * RET_DK) // RET_HEADS
    v_blk0 = 2 * RET_HEADS * RET_DK // dv
    per_batch_state = s0.shape[0] == b and b > 1
    s0_map = (lambda bi, h, c, lg: (bi, h, 0, 0)) if per_batch_state else (lambda bi, h, c, lg: (0, h, 0, 0))
    grid_spec = pltpu.PrefetchScalarGridSpec(
        num_scalar_prefetch=1,
        grid=(b, RET_HEADS, t // chunk),
        in_specs=[
            pl.BlockSpec((1, chunk, RET_DK), lambda bi, h, c, lg: (bi, c, h)),
            pl.BlockSpec((1, chunk, RET_DK), lambda bi, h, c, lg: (bi, c, RET_HEADS + h)),
            pl.BlockSpec((1, chunk, dv), lambda bi, h, c, lg: (bi, c, v_blk0 + h)),
            pl.BlockSpec((1, 1, RET_DK, dv), s0_map),
        ],
        out_specs=[
            pl.BlockSpec((1, chunk, dv), lambda bi, h, c, lg: (bi, c, h)),
            pl.BlockSpec((1, 1, RET_DK, dv), lambda bi, h, c, lg: (bi, h, 0, 0)),
        ],
        scratch_shapes=[pltpu.VMEM((RET_DK, dv), F32)],
    )
    return pl.pallas_call(
        functools.partial(_retention_kernel, n_valid=n_valid),
        grid_spec=grid_spec,
        out_shape=[
            jax.ShapeDtypeStruct((b, t, RET_HEADS * dv), F32),
            jax.ShapeDtypeStruct((b, RET_HEADS, RET_DK, dv), F32),
        ],
        compiler_params=_params(("parallel", "parallel", "arbitrary")),
        name="retention",
    )(log_g, qkv, qkv, qkv, s0)


def _lambda_value(lam_ref, lam_init):
    lp = lam_ref[...]
    s1 = jnp.sum(lp[0:1] * lp[1:2], axis=-1, keepdims=True)
    s2 = jnp.sum(lp[2:3] * lp[3:4], axis=-1, keepdims=True)
    return jnp.exp(s1) - jnp.exp(s2) + lam_init


def _online_update(t, v, m_ref, l_ref, acc_ref):
    m_old = m_ref[...]
    m_new = jnp.maximum(m_old, jnp.max(t, axis=-1, keepdims=True))
    alpha = jnp.exp(m_old - m_new)
    p = jnp.exp(t - m_new)
    l_ref[...] = alpha * l_ref[...] + jnp.sum(p, axis=-1, keepdims=True)
    acc_ref[...] = alpha * acc_ref[...] + jnp.dot(p.astype(BF16), v, preferred_element_type=F32)
    m_ref[...] = m_new


def _prompt_attn_kernel(slope_ref, lam_ref, q_ref, k_ref, v_ref, km_ref, vm_ref, o_ref,
                        m1, l1, a1, m2, l2, a2, *, lam_init):
    h = pl.program_id(1)
    qi = pl.program_id(2)
    tq = q_ref.shape[1]
    n_meta_pad = km_ref.shape[0]
    slope = slope_ref[h]

    lane = lax.broadcasted_iota(jnp.int32, (tq, HEAD_W), 1)
    qs = q_ref[0].astype(F32) * (DIFF_DH ** -0.5)
    q1 = jnp.where(lane < DIFF_DH, qs, 0.0).astype(BF16)
    q2 = jnp.where(lane >= DIFF_DH, qs, 0.0).astype(BF16)

    for m_ref, l_ref, a_ref in ((m1, l1, a1), (m2, l2, a2)):
        m_ref[...] = jnp.full(m_ref.shape, -jnp.inf, F32)
        l_ref[...] = jnp.zeros(l_ref.shape, F32)
        a_ref[...] = jnp.zeros(a_ref.shape, F32)

    def scores(kc):
        return (lax.dot_general(q1, kc, NT_DIMS, preferred_element_type=F32),
                lax.dot_general(q2, kc, NT_DIMS, preferred_element_type=F32))

    row_m = lax.broadcasted_iota(jnp.int32, (tq, n_meta_pad), 0)
    col_m = lax.broadcasted_iota(jnp.int32, (tq, n_meta_pad), 1)
    dist_m = (row_m - col_m + (N_META + qi * tq)).astype(F32)
    s1, s2 = scores(km_ref[...])
    valid = col_m < N_META
    vm = vm_ref[...]
    _online_update(jnp.where(valid, s1 - slope * dist_m, -jnp.inf), vm, m1, l1, a1)
    _online_update(jnp.where(valid, s2 - slope * dist_m, -jnp.inf), vm, m2, l2, a2)

    row = lax.broadcasted_iota(jnp.int32, (tq, tq), 0)
    col = lax.broadcasted_iota(jnp.int32, (tq, tq), 1)
    rel = row - col
    bias = -slope * rel.astype(F32)

    def full_chunk(c, carry):
        start = pl.multiple_of(c * tq, tq)
        kc = k_ref[0, pl.ds(start, tq), :]
        vc = v_ref[0, pl.ds(start, tq), :]
        shift = slope * ((qi - c) * tq).astype(F32)
        s1, s2 = scores(kc)
        _online_update(s1 + bias - shift, vc, m1, l1, a1)
        _online_update(s2 + bias - shift, vc, m2, l2, a2)
        return carry

    lax.fori_loop(0, qi, full_chunk, 0)

    start = pl.multiple_of(qi * tq, tq)
    kc = k_ref[0, pl.ds(start, tq), :]
    vc = v_ref[0, pl.ds(start, tq), :]
    s1, s2 = scores(kc)
    causal = rel >= 0
    _online_update(jnp.where(causal, s1 + bias, -jnp.inf), vc, m1, l1, a1)
    _online_update(jnp.where(causal, s2 + bias, -jnp.inf), vc, m2, l2, a2)

    lam = _lambda_value(lam_ref, lam_init)
    o_ref[0] = a1[...] / l1[...] - lam * (a2[...] / l2[...])


def prompt_attention(slopes, lam_params, q, kv, kv_meta, lam_init, tq):
    b, t, _ = q.shape
    n_meta_pad = kv_meta.shape[0]
    grid_spec = pltpu.PrefetchScalarGridSpec(
        num_scalar_prefetch=1,
        grid=(b, DIFF_HEADS, t // tq),
        in_specs=[
            pl.BlockSpec(lam_params.shape, lambda bi, h, qi, sl: (0, 0)),
            pl.BlockSpec((1, tq, HEAD_W), lambda bi, h, qi, sl: (bi, qi, h)),
            pl.BlockSpec((1, t, HEAD_W), lambda bi, h, qi, sl: (bi, 0, h)),
            pl.BlockSpec((1, t, HEAD_W), lambda bi, h, qi, sl: (bi, 0, DIFF_HEADS + h)),
            pl.BlockSpec((n_meta_pad, HEAD_W), lambda bi, h, qi, sl: (0, h)),
            pl.BlockSpec((n_meta_pad, HEAD_W), lambda bi, h, qi, sl: (0, DIFF_HEADS + h)),
        ],
        out_specs=pl.BlockSpec((1, tq, HEAD_W), lambda bi, h, qi, sl: (bi, qi, h)),
        scratch_shapes=[
            pltpu.VMEM((tq, 1), F32), pltpu.VMEM((tq, 1), F32), pltpu.VMEM((tq, HEAD_W), F32),
            pltpu.VMEM((tq, 1), F32), pltpu.VMEM((tq, 1), F32), pltpu.VMEM((tq, HEAD_W), F32),
        ],
    )
    return pl.pallas_call(
        functools.partial(_prompt_attn_kernel, lam_init=lam_init),
        grid_spec=grid_spec,
        out_shape=jax.ShapeDtypeStruct((b, t, DIFF_HEADS * HEAD_W), F32),
        compiler_params=_params(("parallel", "parallel", "arbitrary")),
        name="prompt_attn",
    )(slopes, lam_params, q, kv, kv, kv_meta, kv_meta)


def _decode_attn_kernel(pt_ref, slope_ref, lam_ref, q_ref, kc_ref, vc_ref, kn_ref, vn_ref, o_ref,
                        m_ref, l_ref, a_ref, *, lam_init, t_new, past):
    p = pl.program_id(1)
    n_pages = pl.num_programs(1)
    rows = q_ref.shape[1]
    page = kc_ref.shape[0]

    @pl.when(p == 0)
    def _():
        m_ref[...] = jnp.full(m_ref.shape, -jnp.inf, F32)
        l_ref[...] = jnp.zeros(l_ref.shape, F32)
        a_ref[...] = jnp.zeros(a_ref.shape, F32)

    row = lax.broadcasted_iota(jnp.int32, (rows, HEAD_W), 0)
    lane = lax.broadcasted_iota(jnp.int32, (rows, HEAD_W), 1)
    keep = ((row < t_new) & (lane < DIFF_DH)) | ((row >= t_new) & (row < 2 * t_new) & (lane >= DIFF_DH))

    def token_of_row(shape):
        r = lax.broadcasted_iota(jnp.int32, shape, 0)
        return jnp.where(r < t_new, r, r - t_new)

    def head_q(h):
        qh = q_ref[0, :, h * HEAD_W:(h + 1) * HEAD_W] * (DIFF_DH ** -0.5)
        return jnp.where(keep, qh, 0.0).astype(BF16)

    def update(h, t, vh):
        m_old = m_ref[h]
        m_new = jnp.maximum(m_old, jnp.max(t, axis=-1, keepdims=True))
        alpha = jnp.exp(m_old - m_new)
        pr = jnp.exp(t - m_new)
        l_ref[h] = alpha * l_ref[h] + jnp.sum(pr, axis=-1, keepdims=True)
        a_ref[h] = alpha * a_ref[h] + jnp.dot(pr.astype(BF16), vh, preferred_element_type=F32)
        m_ref[h] = m_new

    col = lax.broadcasted_iota(jnp.int32, (rows, page), 1)
    dist = (past + token_of_row((rows, page)) - (p * page + col)).astype(F32)
    for h in range(DIFF_HEADS):
        kh = kc_ref[:, h * HEAD_W:(h + 1) * HEAD_W].astype(BF16)
        vh = vc_ref[:, h * HEAD_W:(h + 1) * HEAD_W].astype(BF16)
        s = lax.dot_general(head_q(h), kh, NT_DIMS, preferred_element_type=F32)
        update(h, s - slope_ref[h] * dist, vh)

    @pl.when(p == n_pages - 1)
    def _():
        n_new = kn_ref.shape[1]
        r_n = token_of_row((rows, n_new))
        c_n = lax.broadcasted_iota(jnp.int32, (rows, n_new), 1)
        visible = (c_n < t_new) & (c_n <= r_n)
        dist_n = (r_n - c_n).astype(F32)
        lam = _lambda_value(lam_ref, lam_init)
        for h in range(DIFF_HEADS):
            kh = kn_ref[0, :, h * HEAD_W:(h + 1) * HEAD_W].astype(BF16)
            vh = vn_ref[0, :, h * HEAD_W:(h + 1) * HEAD_W].astype(BF16)
            s = lax.dot_general(head_q(h), kh, NT_DIMS, preferred_element_type=F32)
            update(h, jnp.where(visible, s - slope_ref[h] * dist_n, -jnp.inf), vh)
            n = a_ref[h] / l_ref[h]
            o_ref[0, :, h * HEAD_W:(h + 1) * HEAD_W] = n - lam * pltpu.roll(n, rows - t_new, 0)


def decode_attention(page_table, slopes, lam_params, q, cache_k, cache_v, layer, kv_new, lam_init, t_new):
    b, rows, w = q.shape
    page = cache_k.shape[2]
    n_pages = page_table.shape[1]
    cache_spec = pl.BlockSpec((None, None, page, w), lambda bi, p, pt, sl: (layer, pt[bi, p], 0, 0))
    grid_spec = pltpu.PrefetchScalarGridSpec(
        num_scalar_prefetch=2,
        grid=(b, n_pages),
        in_specs=[
            pl.BlockSpec(lam_params.shape, lambda bi, p, pt, sl: (0, 0)),
            pl.BlockSpec((1, rows, w), lambda bi, p, pt, sl: (bi, 0, 0)),
            cache_spec,
            cache_spec,
            pl.BlockSpec((1, rows, w), lambda bi, p, pt, sl: (bi, 0, 0)),
            pl.BlockSpec((1, rows, w), lambda bi, p, pt, sl: (bi, 0, 1)),
        ],
        out_specs=pl.BlockSpec((1, rows, w), lambda bi, p, pt, sl: (bi, 0, 0)),
        scratch_shapes=[
            pltpu.VMEM((DIFF_HEADS, rows, 1), F32),
            pltpu.VMEM((DIFF_HEADS, rows, 1), F32),
            pltpu.VMEM((DIFF_HEADS, rows, HEAD_W), F32),
        ],
    )
    return pl.pallas_call(
        functools.partial(_decode_attn_kernel, lam_init=lam_init, t_new=t_new, past=n_pages * page),
        grid_spec=grid_spec,
        out_shape=jax.ShapeDtypeStruct((b, rows, w), F32),
        compiler_params=_params(("parallel", "arbitrary")),
        name="decode_attn",
    )(page_table, slopes, lam_params, q, cache_k, cache_v, kv_new, kv_new)


def _merge_kernel(x_ref, oret_ref, odiff_ref, g_ref, rnw_ref, dnw_ref, wout_ref, pnw_ref, o_ref, m_ref,
                  *, lam_init):
    d = x_ref.shape[1]
    ret_dv = rnw_ref.shape[1]
    for hh in range(d // ret_dv):
        sl = slice(hh * ret_dv, (hh + 1) * ret_dv)
        rg = g_ref[:, sl]
        a = _rms(oret_ref[:, sl], rnw_ref[...]) * (rg * jax.nn.sigmoid(rg))
        m_ref[:, sl] = jax.nn.sigmoid(g_ref[:, d + hh * ret_dv:d + (hh + 1) * ret_dv]) * a
    diff_dv = dnw_ref.shape[1]
    for hh in range(d // diff_dv):
        sl = slice(hh * diff_dv, (hh + 1) * diff_dv)
        dd = _rms(odiff_ref[:, sl], dnw_ref[...]) * (1.0 - lam_init)
        gb = g_ref[:, 2 * d + hh * diff_dv:2 * d + (hh + 1) * diff_dv]
        m_ref[:, sl] = m_ref[:, sl] + jax.nn.sigmoid(gb) * dd
    mix = jnp.dot(m_ref[...].astype(BF16), wout_ref[...], preferred_element_type=F32)
    o_ref[...] = x_ref[...] + _rms(mix, pnw_ref[...])


def merge(x, o_ret, o_diff, gates, ret_nw, diff_nw, w_out, post_nw, lam_init, tm):
    t, d = x.shape
    tm = min(tm, t)
    row = lambda i: (i, 0)
    return pl.pallas_call(
        functools.partial(_merge_kernel, lam_init=lam_init),
        grid=(t // tm,),
        in_specs=[
            pl.BlockSpec((tm, d), row), pl.BlockSpec((tm, d), row), pl.BlockSpec((tm, d), row),
            pl.BlockSpec((tm, 3 * d), row),
            _resident((1, ret_nw.shape[0])), _resident((1, diff_nw.shape[0])),
            _resident((d, d)), _resident((1, d)),
        ],
        out_specs=pl.BlockSpec((tm, d), row),
        out_shape=jax.ShapeDtypeStruct((t, d), F32),
        scratch_shapes=[pltpu.VMEM((tm, d), F32)],
        compiler_params=_params(("parallel",)),
        name="merge",
    )(x, o_ret, o_diff, gates, ret_nw.reshape(1, -1), diff_nw.reshape(1, -1), w_out, post_nw.reshape(1, -1))


def _ffn_kernel(x_ref, pre_ref, wg_ref, wu_ref, wd_ref, post_ref, o_ref, *, n_split):
    x = x_ref[...]
    h = _rms(x, pre_ref[...]).astype(BF16)
    dff = wg_ref.shape[1]
    step = dff // n_split
    y = jnp.zeros(x.shape, F32)
    for s in range(n_split):
        sl = slice(s * step, (s + 1) * step)
        g = jnp.dot(h, wg_ref[:, sl], preferred_element_type=F32)
        u = jnp.dot(h, wu_ref[:, sl], preferred_element_type=F32)
        act = (g * jax.nn.sigmoid(g) * u).astype(BF16)
        y = y + jnp.dot(act, wd_ref[sl, :], preferred_element_type=F32)
    o_ref[...] = x + _rms(y, post_ref[...])


def ffn(x, pre_nw, w_gate, w_up, w_down, post_nw, tm, n_split):
    t, d = x.shape
    dff = w_gate.shape[1]
    tm = min(tm, t)
    row = lambda i: (i, 0)
    return pl.pallas_call(
        functools.partial(_ffn_kernel, n_split=n_split),
        grid=(t // tm,),
        in_specs=[
            pl.BlockSpec((tm, d), row), _resident((1, d)),
            _resident((d, dff)), _resident((d, dff)), _resident((dff, d)),
            _resident((1, d)),
        ],
        out_specs=pl.BlockSpec((tm, d), row),
        out_shape=jax.ShapeDtypeStruct((t, d), F32),
        compiler_params=_params(("parallel",)),
        name="ffn",
    )(x, pre_nw.reshape(1, -1), w_gate, w_up, w_down, post_nw.reshape(1, -1))


def _project(x, nw, w_ret, w_gate, w_q, w_kv, tm):
    (ret,) = norm_matmul(x, nw, w_ret, (BF16,), tm, 1024)
    (gates,) = norm_matmul(x, nw, w_gate, (F32,), tm, 1024)
    (q,) = norm_matmul(x, nw, w_q, (BF16,), tm, 1024)
    kv32, kv16 = norm_matmul(x, nw, w_kv, (F32, BF16), tm, 1024)
    return ret, gates, q, kv32, kv16


def kernel(x_prompt, x_sample, cache_k, cache_v, state_ret, page_table, meta_tokens, norm_mix_pre,
           norm_mix_post, w_in, ret_norm, lambda_q1, lambda_k1, lambda_q2, lambda_k2, diff_norm, w_out,
           norm_ffn_pre, norm_ffn_post, w_ffn_gate, w_ffn_up, w_ffn_down):
    b, seq, d = x_prompt.shape
    db, t_new, _ = x_sample.shape
    depth, n_pool, page = cache_k.shape[:3]
    ret_qk = RET_HEADS * RET_DK
    kv_w = DIFF_HEADS * HEAD_W
    n_meta = meta_tokens.shape[0]
    assert n_meta == N_META
    log_g = jnp.log1p(-jnp.exp2(-5.0 - jnp.arange(RET_HEADS, dtype=F32)))
    slopes = jnp.exp2(-8.0 * (jnp.arange(DIFF_HEADS, dtype=F32) + 1.0) / DIFF_HEADS)
    pad_rows = 16
    meta_pad = 128
    cache_k = cache_k.reshape(depth, n_pool, page, kv_w)
    cache_v = cache_v.reshape(depth, n_pool, page, kv_w)

    xp = x_prompt.reshape(b * seq, d)
    xs = x_sample.reshape(db * t_new, d)
    outs = [[] for _ in range(6)]
    for l in range(depth):
        lam_init = 0.8 - 0.6 * math.exp(-0.3 * l)
        lam_params = jnp.stack([lambda_q1[l], lambda_k1[l], lambda_q2[l], lambda_k2[l]]).astype(F32)
        wl = w_in[l].astype(BF16)
        o0 = 2 * ret_qk + 2 * d
        w_ret = wl[:, :2 * ret_qk + d]
        w_gate = jnp.concatenate([wl[:, 2 * ret_qk + d:o0], wl[:, o0 + 3 * kv_w:]], axis=1)
        w_q = wl[:, o0:o0 + kv_w]
        w_kv = wl[:, o0 + kv_w:o0 + 3 * kv_w]
        w_o = w_out[l].astype(BF16)
        w_g, w_u, w_d = (w.astype(BF16) for w in (w_ffn_gate[l], w_ffn_up[l], w_ffn_down[l]))
        proj = functools.partial(_project, nw=norm_mix_pre[l], w_ret=w_ret, w_gate=w_gate, w_q=w_q, w_kv=w_kv)

        def finish(x, o_ret, o_diff, gates, tm):
            x = merge(x, o_ret, o_diff, gates, ret_norm[l], diff_norm[l], w_o, norm_mix_post[l], lam_init, tm)
            return ffn(x, norm_ffn_pre[l], w_g, w_u, w_d, norm_ffn_post[l], tm, 2)

        ret_m, _, _, kv32_m, kv16_m = proj(meta_tokens.astype(F32), tm=n_meta)
        zero_state = jnp.zeros((1, RET_HEADS, RET_DK, d // RET_HEADS), F32)
        _, s_meta = retention(log_g, ret_m[None], zero_state, n_meta, n_meta)
        kv16_m = jnp.pad(kv16_m, ((0, meta_pad - n_meta), (0, 0)))

        ret_p, gates_p, q_p, kv32_p, kv16_p = proj(xp, tm=1024)
        o_ret, s_fin = retention(log_g, ret_p.reshape(b, seq, -1), s_meta, 128, 128)
        o_diff = prompt_attention(slopes, lam_params, q_p.reshape(b, seq, kv_w),
                                  kv16_p.reshape(b, seq, 2 * kv_w), kv16_m, lam_init, 256)
        xp = finish(xp, o_ret.reshape(b * seq, d), o_diff.reshape(b * seq, d), gates_p, 512)
        kv32_p = kv32_p.reshape(b, seq, 2 * kv_w)
        for o, lo in ((outs[0], 0), (outs[1], kv_w)):
            meta_part = jnp.broadcast_to(kv32_m[None, :, lo:lo + kv_w], (b, n_meta, kv_w))
            both = jnp.concatenate([meta_part, kv32_p[:, :, lo:lo + kv_w]], axis=1)
            o.append(both.reshape(b, n_meta + seq, DIFF_HEADS, HEAD_W))
        outs[2].append(s_fin)

        ret_s, gates_s, q_s, kv32_s, _ = proj(xs, tm=512)
        pad3 = lambda a, n: jnp.pad(a.reshape(db, t_new, -1), ((0, 0), (0, n - t_new), (0, 0)))
        o_ret_s, s_new = retention(log_g, pad3(ret_s, pad_rows), state_ret[l], pad_rows, t_new)
        q_s = q_s.astype(F32).reshape(db, t_new, kv_w)
        q_rows = jnp.pad(jnp.concatenate([q_s, q_s], axis=1), ((0, 0), (0, pad_rows - 2 * t_new), (0, 0)))
        o_diff_s = decode_attention(page_table, slopes, lam_params, q_rows, cache_k, cache_v, l,
                                    pad3(kv32_s, pad_rows), lam_init, t_new)
        xs = finish(xs, o_ret_s[:, :t_new].reshape(db * t_new, d), o_diff_s[:, :t_new].reshape(db * t_new, d),
                    gates_s, 512)
        kv32_s = kv32_s.reshape(db, t_new, 2 * kv_w)
        outs[3].append(kv32_s[:, :, :kv_w].reshape(db, t_new, DIFF_HEADS, HEAD_W))
        outs[4].append(kv32_s[:, :, kv_w:].reshape(db, t_new, DIFF_HEADS, HEAD_W))
        outs[5].append(s_new)

    kp, vp, sp, ks, vs, ss = (jnp.stack(o) for o in outs)
    return (xp.reshape(b, seq, d), xs.reshape(db, t_new, d), kp, vp, sp, ks, vs, ss)
```

```python
import functools
import math

import jax
import jax.numpy as jnp
from jax import lax
from jax.experimental import pallas as pl
from jax.experimental.pallas import tpu as pltpu

F32 = jnp.float32
BF16 = jnp.bfloat16

RMS_EPS = 1e-6
N_META = 16
RET_HEADS = 4
RET_DK = 128
DIFF_HEADS = 8
DIFF_DH = 64
HEAD_W = 2 * DIFF_DH
ATT_BLOCK = 256
KV_CHUNK = 1024
SUM_ROWS = 16
DECODE_PAGES_PER_STEP = 4
VMEM_LIMIT = 56 * 1024 * 1024
NT_DIMS = (((1,), (1,)), ((), ()))
TN_DIMS = (((0,), (0,)), ((), ()))


def _params(semantics):
    return pltpu.CompilerParams(dimension_semantics=semantics, vmem_limit_bytes=VMEM_LIMIT)


def _resident(shape):
    return pl.BlockSpec(shape, lambda *_: (0,) * len(shape), pipeline_mode=pl.Buffered(1))


def _div_pow2(x, n):
    assert n & (n - 1) == 0
    return lax.shift_right_logical(x, n.bit_length() - 1)


def _mod_pow2(x, n):
    assert n & (n - 1) == 0
    return x & (n - 1)


def _rms(x, w):
    ms = jnp.mean(x * x, axis=-1, keepdims=True)
    return x * lax.rsqrt(ms + RMS_EPS) * w


def _norm_matmul_kernel(x_ref, nw_ref, w_ref, *refs):
    out_refs, h_ref = refs[:-1], refs[-1]

    @pl.when(pl.program_id(1) == 0)
    def _():
        h_ref[...] = _rms(x_ref[...], nw_ref[...]).astype(BF16)

    acc = jnp.dot(h_ref[...], w_ref[...], preferred_element_type=F32)
    for o in out_refs:
        o[...] = acc.astype(o.dtype)


def norm_matmul(x, nw, w, out_dtypes, tm, tn):
    t, k = x.shape
    n = w.shape[1]
    tm = min(tm, t)
    tn = min(tn, n)
    assert t % tm == 0 and n % tn == 0
    return pl.pallas_call(
        _norm_matmul_kernel,
        grid=(t // tm, n // tn),
        in_specs=[
            pl.BlockSpec((tm, k), lambda i, j: (i, 0)),
            pl.BlockSpec((1, k), lambda i, j: (0, 0)),
            pl.BlockSpec((k, tn), lambda i, j: (0, j)),
        ],
        out_specs=[pl.BlockSpec((tm, tn), lambda i, j: (i, j)) for _ in out_dtypes],
        out_shape=[jax.ShapeDtypeStruct((t, n), d) for d in out_dtypes],
        scratch_shapes=[pltpu.VMEM((tm, k), BF16)],
        compiler_params=_params(("parallel", "arbitrary")),
        name="norm_matmul",
    )(x, nw.reshape(1, k), w)


def _value_proj_kernel(x_ref, nw_ref, w_ref, o32_ref, vt_ref):
    h = _rms(x_ref[...], nw_ref[...]).astype(BF16)
    acc = jnp.dot(h, w_ref[...], preferred_element_type=F32)
    o32_ref[...] = acc
    heads, chunks, dv, ck = vt_ref.shape
    for hh in range(heads):
        for cc in range(chunks):
            blk = acc[cc * ck:(cc + 1) * ck, hh * dv:(hh + 1) * dv]
            vt_ref[hh, cc] = jnp.transpose(blk).astype(BF16)


def value_proj(x, nw, w, tm, ck):
    t, k = x.shape
    n = w.shape[1]
    assert t % tm == 0 and tm % ck == 0 and n == DIFF_HEADS * HEAD_W
    return pl.pallas_call(
        _value_proj_kernel,
        grid=(t // tm,),
        in_specs=[pl.BlockSpec((tm, k), lambda i: (i, 0)), _resident((1, k)), _resident((k, n))],
        out_specs=[
            pl.BlockSpec((tm, n), lambda i: (i, 0)),
            pl.BlockSpec((DIFF_HEADS, tm // ck, HEAD_W, ck), lambda i: (0, i, 0, 0)),
        ],
        out_shape=[
            jax.ShapeDtypeStruct((t, n), F32),
            jax.ShapeDtypeStruct((DIFF_HEADS, t // ck, HEAD_W, ck), BF16),
        ],
        compiler_params=_params(("parallel",)),
        name="value_proj",
    )(x, nw.reshape(1, k), w)


def _retention_kernel(logg_ref, q_ref, k_ref, v_ref, s0_ref, o_ref, sfin_ref, state_ref, *, n_valid):
    h = pl.program_id(1)
    c = pl.program_id(2)
    chunk = q_ref.shape[1]
    lg = logg_ref[h]

    @pl.when(c == 0)
    def _():
        state_ref[...] = s0_ref[0, 0]

    q = q_ref[0].astype(F32)
    k = k_ref[0].astype(F32) * (RET_DK ** -0.5)
    v = v_ref[0]
    row = lax.broadcasted_iota(jnp.int32, (chunk, chunk), 0)
    col = lax.broadcasted_iota(jnp.int32, (chunk, chunk), 1)
    rel = (row - col).astype(F32)
    decay_in = jnp.where(rel >= 0, jnp.exp(jnp.maximum(rel, 0.0) * lg), 0.0)
    idx = lax.broadcasted_iota(jnp.int32, (chunk, 1), 0).astype(F32)
    q_dec = jnp.exp((idx + 1.0) * lg)
    k_dec = jnp.exp((n_valid - 1.0 - idx) * lg)
    s_dec = jnp.exp(jnp.full((1, 1), float(n_valid), F32) * lg)

    state = state_ref[...]
    scores = lax.dot_general(q.astype(BF16), k.astype(BF16), NT_DIMS, preferred_element_type=F32) * decay_in
    o = jnp.dot(scores.astype(BF16), v, preferred_element_type=F32)
    o = o + jnp.dot((q * q_dec).astype(BF16), state.astype(BF16), preferred_element_type=F32)
    o_ref[0] = o.astype(o_ref.dtype)

    state = s_dec * state + lax.dot_general((k * k_dec).astype(BF16), v, TN_DIMS, preferred_element_type=F32)
    state_ref[...] = state

    @pl.when(c == pl.num_programs(2) - 1)
    def _():
        sfin_ref[0, 0] = state


def retention(log_g, qkv, s0, chunk, n_valid):
    b, t, w = qkv.shape
    dv = (w - 2 * RET_HEADS * RET_DK) // RET_HEADS
    v_blk0 = 2 * RET_HEADS * RET_DK // dv
    per_batch_state = s0.shape[0] == b and b > 1
    s0_map = (lambda bi, h, c, lg: (bi, h, 0, 0)) if per_batch_state else (lambda bi, h, c, lg: (0, h, 0, 0))
    grid_spec = pltpu.PrefetchScalarGridSpec(
        num_scalar_prefetch=1,
        grid=(b, RET_HEADS, t // chunk),
        in_specs=[
            pl.BlockSpec((1, chunk, RET_DK), lambda bi, h, c, lg: (bi, c, h)),
            pl.BlockSpec((1, chunk, RET_DK), lambda bi, h, c, lg: (bi, c, RET_HEADS + h)),
            pl.BlockSpec((1, chunk, dv), lambda bi, h, c, lg: (bi, c, v_blk0 + h)),
            pl.BlockSpec((1, 1, RET_DK, dv), s0_map),
        ],
        out_specs=[
            pl.BlockSpec((1, chunk, dv), lambda bi, h, c, lg: (bi, c, h)),
            pl.BlockSpec((1, 1, RET_DK, dv), lambda bi, h, c, lg: (bi, h, 0, 0)),
        ],
        scratch_shapes=[pltpu.VMEM((RET_DK, dv), F32)],
    )
    return pl.pallas_call(
        functools.partial(_retention_kernel, n_valid=n_valid),
        grid_spec=grid_spec,
        out_shape=[
            jax.ShapeDtypeStruct((b, t, RET_HEADS * dv), F32),
            jax.ShapeDtypeStruct((b, RET_HEADS, RET_DK, dv), F32),
        ],
        compiler_params=_params(("parallel", "parallel", "arbitrary")),
        name="retention",
    )(log_g, qkv, qkv, qkv, s0)


def _lambda_value(lam_ref, lam_init):
    lp = lam_ref[...]
    s1 = jnp.sum(lp[0:1] * lp[1:2], axis=-1, keepdims=True)
    s2 = jnp.sum(lp[2:3] * lp[3:4], axis=-1, keepdims=True)
    return jnp.exp(s1) - jnp.exp(s2) + lam_init


def _prompt_attn_kernel(slope_ref, lam_ref, q_ref, k_ref, vt_ref, km_ref, vtm_ref, o_ref,
                        m_ref, acc_ref, *, lam_init):
    h = pl.program_id(1)
    qi = pl.program_id(2)
    tq = q_ref.shape[1]
    ck = vt_ref.shape[3]
    n_meta_pad = km_ref.shape[0]
    slope = slope_ref[h]

    lane = lax.broadcasted_iota(jnp.int32, (tq, HEAD_W), 1)
    qs = q_ref[0].astype(F32) * (DIFF_DH ** -0.5)
    one_rows = (lax.broadcasted_iota(jnp.int32, (HEAD_W, tq), 0) < 2).astype(F32)
    qa1 = jnp.concatenate([jnp.transpose(jnp.where(lane < DIFF_DH, qs, 0.0)), one_rows], axis=0).astype(BF16)
    qa2 = jnp.concatenate([jnp.transpose(jnp.where(lane >= DIFF_DH, qs, 0.0)), one_rows], axis=0).astype(BF16)
    qa = jnp.concatenate([qa1, qa2], axis=1)

    j_row = lax.broadcasted_iota(jnp.int32, (ck, HEAD_W), 0)
    j_col = lax.broadcasted_iota(jnp.int32, (ck, HEAD_W), 1)
    j_lo = _mod_pow2(j_row, 256).astype(F32)
    j_hi = (j_row - _mod_pow2(j_row, 256)).astype(F32)
    bias_cols = (slope * jnp.where(j_col == 0, j_lo, jnp.where(j_col == 1, j_hi, 0.0))).astype(BF16)
    ones_rows = jnp.ones((SUM_ROWS, ck), BF16)

    m_ref[...] = jnp.full(m_ref.shape, -jnp.inf, F32)
    acc_ref[...] = jnp.zeros(acc_ref.shape, F32)

    def step(k_aug, v_aug, shift, mask):
        s = jnp.dot(k_aug, qa, preferred_element_type=F32)
        if mask is not None:
            s = jnp.where(jnp.concatenate([mask, mask], axis=1), s, -jnp.inf)
        m_old = m_ref[...]
        m_new = jnp.maximum(m_old, jnp.max(s, axis=0, keepdims=True) + shift)
        alpha = jnp.exp(m_old - m_new)
        p = jnp.exp(s - (m_new - shift)).astype(BF16)
        acc_ref[...] = alpha * acc_ref[...] + jnp.dot(v_aug, p, preferred_element_type=F32)
        m_ref[...] = m_new

    def chunk_operands(c):
        start = pl.multiple_of(c * ck, ck)
        k_aug = jnp.concatenate([k_ref[0, pl.ds(start, ck), :], bias_cols], axis=1)
        v_aug = jnp.concatenate([vt_ref[0, c], ones_rows], axis=0)
        return k_aug, v_aug, (qi * tq - c * ck)

    k_aug = jnp.concatenate([km_ref[...], bias_cols[:n_meta_pad]], axis=1)
    v_aug = jnp.concatenate([vtm_ref[...], ones_rows[:, :n_meta_pad]], axis=0)
    key_is_meta = lax.broadcasted_iota(jnp.int32, (n_meta_pad, tq), 0) < N_META
    step(k_aug, v_aug, -slope * (N_META + qi * tq).astype(F32), key_is_meta)

    def full_chunk(c, carry):
        k_aug, v_aug, off = chunk_operands(c)
        step(k_aug, v_aug, -slope * off.astype(F32), None)
        return carry

    n_full = _div_pow2(qi, ck // tq)
    lax.fori_loop(0, n_full, full_chunk, 0)
    k_aug, v_aug, off = chunk_operands(n_full)
    key_minus_query = (lax.broadcasted_iota(jnp.int32, (ck, tq), 0)
                       - lax.broadcasted_iota(jnp.int32, (ck, tq), 1))
    step(k_aug, v_aug, -slope * off.astype(F32), key_minus_query <= off)

    lam = _lambda_value(lam_ref, lam_init)
    n = acc_ref[:HEAD_W, :] / acc_ref[HEAD_W:HEAD_W + 1, :]
    o_t = n[:, :tq] - lam * n[:, tq:]
    o_ref[0] = jnp.transpose(o_t)


def prompt_attention(slopes, lam_params, q, k, vt, k_meta, vt_meta, lam_init):
    b, t, _ = q.shape
    tq = ATT_BLOCK
    ck = vt.shape[3]
    n_meta_pad = k_meta.shape[0]
    assert t % ck == 0 and ck % tq == 0
    grid_spec = pltpu.PrefetchScalarGridSpec(
        num_scalar_prefetch=1,
        grid=(b, DIFF_HEADS, t // tq),
        in_specs=[
            pl.BlockSpec(lam_params.shape, lambda bi, h, qi, sl: (0, 0)),
            pl.BlockSpec((1, tq, HEAD_W), lambda bi, h, qi, sl: (bi, qi, h)),
            pl.BlockSpec((1, t, HEAD_W), lambda bi, h, qi, sl: (bi, 0, h)),
            pl.BlockSpec((1, t // ck, HEAD_W, ck), lambda bi, h, qi, sl: (h, bi, 0, 0)),
            pl.BlockSpec((n_meta_pad, HEAD_W), lambda bi, h, qi, sl: (0, h)),
            pl.BlockSpec((HEAD_W, n_meta_pad), lambda bi, h, qi, sl: (h, 0)),
        ],
        out_specs=pl.BlockSpec((1, tq, HEAD_W), lambda bi, h, qi, sl: (bi, qi, h)),
        scratch_shapes=[
            pltpu.VMEM((1, 2 * tq), F32), pltpu.VMEM((HEAD_W + SUM_ROWS, 2 * tq), F32),
        ],
    )
    return pl.pallas_call(
        functools.partial(_prompt_attn_kernel, lam_init=lam_init),
        grid_spec=grid_spec,
        out_shape=jax.ShapeDtypeStruct((b, t, DIFF_HEADS * HEAD_W), F32),
        compiler_params=_params(("parallel", "parallel", "arbitrary")),
        name="prompt_attn",
    )(slopes, lam_params, q, k, vt, k_meta, vt_meta)


def _decode_attn_kernel(pt_ref, lam_ref, srow_ref, q_ref, *refs, lam_init, t_new, past, n_par):
    k_refs, v_refs = refs[:n_par], refs[n_par:2 * n_par]
    kn_ref, vn_ref, o_ref, m_ref, l_ref, a_ref, b0_ref = refs[2 * n_par:]
    g = pl.program_id(1)
    rows = q_ref.shape[1]
    page, heads = k_refs[0].shape[0], k_refs[0].shape[1]
    nk = page * heads
    srow = srow_ref[:, 0:1]

    def row_parts(shape):
        r = lax.broadcasted_iota(jnp.int32, shape, 0)
        return _div_pow2(r, 2 * t_new), _mod_pow2(r, t_new)

    def col_parts(shape):
        c = lax.broadcasted_iota(jnp.int32, shape, 1)
        return _div_pow2(c, heads), _mod_pow2(c, heads)

    @pl.when(g == 0)
    def _():
        m_ref[...] = jnp.full(m_ref.shape, -jnp.inf, F32)
        l_ref[...] = jnp.zeros(l_ref.shape, F32)
        a_ref[...] = jnp.zeros(a_ref.shape, F32)
        r_head, r_tok = row_parts((rows, n_par * nk))
        c_tok, c_head = col_parts((rows, n_par * nk))
        b0_ref[...] = jnp.where(r_head == c_head, srow * (c_tok - r_tok).astype(F32), -jnp.inf)

    row = lax.broadcasted_iota(jnp.int32, (rows, HEAD_W), 0)
    lane = lax.broadcasted_iota(jnp.int32, (rows, HEAD_W), 1)
    first_map = _mod_pow2(_div_pow2(row, t_new), 2) == 0
    qa = jnp.where(first_map == (lane < DIFF_DH), q_ref[0] * (DIFF_DH ** -0.5), 0.0).astype(BF16)

    def update(t, shift, v16):
        m_old = m_ref[...]
        m_new = jnp.maximum(m_old, jnp.max(t, axis=-1, keepdims=True) + shift)
        alpha = jnp.exp(m_old - m_new)
        pr = jnp.exp(t - (m_new - shift))
        l_ref[...] = alpha * l_ref[...] + jnp.sum(pr, axis=-1, keepdims=True)
        a_ref[...] = alpha * a_ref[...] + jnp.dot(pr.astype(BF16), v16, preferred_element_type=F32)
        m_ref[...] = m_new

    k16 = jnp.concatenate([r[...].reshape(nk, HEAD_W).astype(BF16) for r in k_refs], axis=0)
    v16 = jnp.concatenate([r[...].reshape(nk, HEAD_W).astype(BF16) for r in v_refs], axis=0)
    s = lax.dot_general(qa, k16, NT_DIMS, preferred_element_type=F32)
    update(s + b0_ref[...], -srow * (past - g * (n_par * page)).astype(F32), v16)

    @pl.when(g == pl.num_programs(1) - 1)
    def _():
        n_new = kn_ref.shape[1]
        r_head, r_tok = row_parts((rows, n_new))
        c_tok, c_head = col_parts((rows, n_new))
        visible = (r_head == c_head) & (c_tok <= r_tok)
        s = lax.dot_general(qa, kn_ref[0].astype(BF16), NT_DIMS, preferred_element_type=F32)
        t = jnp.where(visible, s + srow * (c_tok - r_tok).astype(F32), -jnp.inf)
        update(t, jnp.zeros((), F32), vn_ref[0].astype(BF16))
        n = a_ref[...] / l_ref[...]
        lam = _lambda_value(lam_ref, lam_init)
        o_ref[0] = n - lam * pltpu.roll(n, rows - t_new, 0)


def decode_attention(page_table, lam_params, slope_rows, q_rows, cache_k, cache_v, layer, k_new, v_new,
                     lam_init, t_new):
    b, rows, _ = q_rows.shape
    page, heads = cache_k.shape[2], cache_k.shape[3]
    n_pages = page_table.shape[1]
    n_par = DECODE_PAGES_PER_STEP
    assert n_pages % n_par == 0

    def cache_spec(i):
        return pl.BlockSpec((None, None, page, heads, HEAD_W),
                            lambda bi, g, pt: (layer, pt[bi, g * n_par + i], 0, 0, 0))

    per_seq = lambda bi, g, pt: (bi, 0, 0)
    grid_spec = pltpu.PrefetchScalarGridSpec(
        num_scalar_prefetch=1,
        grid=(b, n_pages // n_par),
        in_specs=[
            pl.BlockSpec(lam_params.shape, lambda bi, g, pt: (0, 0)),
            pl.BlockSpec(slope_rows.shape, lambda bi, g, pt: (0, 0)),
            pl.BlockSpec((1, rows, HEAD_W), per_seq),
            *[cache_spec(i) for i in range(n_par)],
            *[cache_spec(i) for i in range(n_par)],
            pl.BlockSpec((1,) + k_new.shape[1:], per_seq),
            pl.BlockSpec((1,) + v_new.shape[1:], per_seq),
        ],
        out_specs=pl.BlockSpec((1, rows, HEAD_W), per_seq),
        scratch_shapes=[
            pltpu.VMEM((rows, 1), F32), pltpu.VMEM((rows, 1), F32), pltpu.VMEM((rows, HEAD_W), F32),
            pltpu.VMEM((rows, n_par * page * heads), F32),
        ],
    )
    return pl.pallas_call(
        functools.partial(_decode_attn_kernel, lam_init=lam_init, t_new=t_new, past=n_pages * page, n_par=n_par),
        grid_spec=grid_spec,
        out_shape=jax.ShapeDtypeStruct((b, rows, HEAD_W), F32),
        compiler_params=_params(("parallel", "arbitrary")),
        name="decode_attn",
    )(page_table, lam_params, slope_rows, q_rows, *([cache_k] * n_par), *([cache_v] * n_par), k_new, v_new)


def _merge_kernel(x_ref, oret_ref, odiff_ref, g_ref, rnw_ref, dnw_ref, wout_ref, pnw_ref, o_ref, m_ref,
                  *, lam_init):
    d = x_ref.shape[1]
    ret_dv = rnw_ref.shape[1]
    for hh in range(d // ret_dv):
        sl = slice(hh * ret_dv, (hh + 1) * ret_dv)
        rg = g_ref[:, sl]
        a = _rms(oret_ref[:, sl], rnw_ref[...]) * (rg * jax.nn.sigmoid(rg))
        m_ref[:, sl] = jax.nn.sigmoid(g_ref[:, d + hh * ret_dv:d + (hh + 1) * ret_dv]) * a
    diff_dv = dnw_ref.shape[1]
    for hh in range(d // diff_dv):
        sl = slice(hh * diff_dv, (hh + 1) * diff_dv)
        dd = _rms(odiff_ref[:, sl], dnw_ref[...]) * (1.0 - lam_init)
        gb = g_ref[:, 2 * d + hh * diff_dv:2 * d + (hh + 1) * diff_dv]
        m_ref[:, sl] = m_ref[:, sl] + jax.nn.sigmoid(gb) * dd
    mix = jnp.dot(m_ref[...].astype(BF16), wout_ref[...], preferred_element_type=F32)
    o_ref[...] = x_ref[...] + _rms(mix, pnw_ref[...])


def merge(x, o_ret, o_diff, gates, ret_nw, diff_nw, w_out, post_nw, lam_init, tm):
    t, d = x.shape
    tm = min(tm, t)
    row = lambda i: (i, 0)
    return pl.pallas_call(
        functools.partial(_merge_kernel, lam_init=lam_init),
        grid=(t // tm,),
        in_specs=[
            pl.BlockSpec((tm, d), row), pl.BlockSpec((tm, d), row), pl.BlockSpec((tm, d), row),
            pl.BlockSpec((tm, 3 * d), row),
            _resident((1, ret_nw.shape[0])), _resident((1, diff_nw.shape[0])),
            _resident((d, d)), _resident((1, d)),
        ],
        out_specs=pl.BlockSpec((tm, d), row),
        out_shape=jax.ShapeDtypeStruct((t, d), F32),
        scratch_shapes=[pltpu.VMEM((tm, d), F32)],
        compiler_params=_params(("parallel",)),
        name="merge",
    )(x, o_ret, o_diff, gates, ret_nw.reshape(1, -1), diff_nw.reshape(1, -1), w_out, post_nw.reshape(1, -1))


def _ffn_kernel(x_ref, pre_ref, wg_ref, wu_ref, wd_ref, post_ref, o_ref, *, n_split):
    x = x_ref[...]
    h = _rms(x, pre_ref[...]).astype(BF16)
    dff = wg_ref.shape[1]
    step = dff // n_split
    y = jnp.zeros(x.shape, F32)
    for s in range(n_split):
        sl = slice(s * step, (s + 1) * step)
        g = jnp.dot(h, wg_ref[:, sl], preferred_element_type=F32)
        u = jnp.dot(h, wu_ref[:, sl], preferred_element_type=F32)
        act = (g * jax.nn.sigmoid(g) * u).astype(BF16)
        y = y + jnp.dot(act, wd_ref[sl, :], preferred_element_type=F32)
    o_ref[...] = x + _rms(y, post_ref[...])


def ffn(x, pre_nw, w_gate, w_up, w_down, post_nw, tm, n_split):
    t, d = x.shape
    dff = w_gate.shape[1]
    tm = min(tm, t)
    row = lambda i: (i, 0)
    return pl.pallas_call(
        functools.partial(_ffn_kernel, n_split=n_split),
        grid=(t // tm,),
        in_specs=[
            pl.BlockSpec((tm, d), row), _resident((1, d)),
            _resident((d, dff)), _resident((d, dff)), _resident((dff, d)),
            _resident((1, d)),
        ],
        out_specs=pl.BlockSpec((tm, d), row),
        out_shape=jax.ShapeDtypeStruct((t, d), F32),
        compiler_params=_params(("parallel",)),
        name="ffn",
    )(x, pre_nw.reshape(1, -1), w_gate, w_up, w_down, post_nw.reshape(1, -1))


def kernel(x_prompt, x_sample, cache_k, cache_v, state_ret, page_table, meta_tokens, norm_mix_pre,
           norm_mix_post, w_in, ret_norm, lambda_q1, lambda_k1, lambda_q2, lambda_k2, diff_norm, w_out,
           norm_ffn_pre, norm_ffn_post, w_ffn_gate, w_ffn_up, w_ffn_down):
    b, seq, d = x_prompt.shape
    db, t_new, _ = x_sample.shape
    depth = cache_k.shape[0]
    ret_qk = RET_HEADS * RET_DK
    kv_w = DIFF_HEADS * HEAD_W
    n_meta = meta_tokens.shape[0]
    assert n_meta == N_META
    log_g = jnp.log1p(-jnp.exp2(-5.0 - jnp.arange(RET_HEADS, dtype=F32)))
    slopes = jnp.exp2(-8.0 * (jnp.arange(DIFF_HEADS, dtype=F32) + 1.0) / DIFF_HEADS)
    pad_rows = 16
    meta_pad = 128
    new_pad = 128
    slope_rows = jnp.broadcast_to(jnp.repeat(slopes, 2 * t_new)[:, None], (DIFF_HEADS * 2 * t_new, 128))

    xp = x_prompt.reshape(b * seq, d)
    xs = x_sample.reshape(db * t_new, d)
    outs = [[] for _ in range(6)]
    for l in range(depth):
        lam_init = 0.8 - 0.6 * math.exp(-0.3 * l)
        lam_params = jnp.stack([lambda_q1[l], lambda_k1[l], lambda_q2[l], lambda_k2[l]]).astype(F32)
        wl = w_in[l].astype(BF16)
        o0 = 2 * ret_qk + 2 * d
        w_ret = wl[:, :2 * ret_qk + d]
        w_gate = jnp.concatenate([wl[:, 2 * ret_qk + d:o0], wl[:, o0 + 3 * kv_w:]], axis=1)
        w_q = wl[:, o0:o0 + kv_w]
        w_k = wl[:, o0 + kv_w:o0 + 2 * kv_w]
        w_v = wl[:, o0 + 2 * kv_w:o0 + 3 * kv_w]
        w_kv = wl[:, o0 + kv_w:o0 + 3 * kv_w]
        w_o = w_out[l].astype(BF16)
        w_g, w_u, w_d = (w.astype(BF16) for w in (w_ffn_gate[l], w_ffn_up[l], w_ffn_down[l]))
        nw = norm_mix_pre[l]

        def finish(x, o_ret, o_diff, gates, tm):
            x = merge(x, o_ret, o_diff, gates, ret_norm[l], diff_norm[l], w_o, norm_mix_post[l], lam_init, tm)
            return ffn(x, norm_ffn_pre[l], w_g, w_u, w_d, norm_ffn_post[l], tm, 2)

        xm = meta_tokens.astype(F32)
        (ret_m,) = norm_matmul(xm, nw, w_ret, (BF16,), n_meta, 1024)
        (kv32_m,) = norm_matmul(xm, nw, w_kv, (F32,), n_meta, 1024)
        zero_state = jnp.zeros((1, RET_HEADS, RET_DK, d // RET_HEADS), F32)
        _, s_meta = retention(log_g, ret_m[None], zero_state, n_meta, n_meta)
        kv16_m = jnp.pad(kv32_m.astype(BF16), ((0, meta_pad - n_meta), (0, 0)))
        k_meta, vt_meta = kv16_m[:, :kv_w], kv16_m[:, kv_w:].T

        (ret_p,) = norm_matmul(xp, nw, w_ret, (BF16,), 1024, 1024)
        (gates_p,) = norm_matmul(xp, nw, w_gate, (F32,), 1024, 1024)
        (q_p,) = norm_matmul(xp, nw, w_q, (BF16,), 1024, 1024)
        k32_p, k16_p = norm_matmul(xp, nw, w_k, (F32, BF16), 1024, 1024)
        v32_p, vt_p = value_proj(xp, nw, w_v, min(KV_CHUNK, seq), min(KV_CHUNK, seq))
        o_ret, s_fin = retention(log_g, ret_p.reshape(b, seq, -1), s_meta, 128, 128)
        o_diff = prompt_attention(slopes, lam_params, q_p.reshape(b, seq, kv_w), k16_p.reshape(b, seq, kv_w),
                                  vt_p, k_meta, vt_meta, lam_init)
        xp = finish(xp, o_ret.reshape(b * seq, d), o_diff.reshape(b * seq, d), gates_p, 512)
        for o, main, lo in ((outs[0], k32_p, 0), (outs[1], v32_p, kv_w)):
            meta_part = jnp.broadcast_to(kv32_m[None, :, lo:lo + kv_w], (b, n_meta, kv_w))
            both = jnp.concatenate([meta_part, main.reshape(b, seq, kv_w)], axis=1)
            o.append(both.reshape(b, n_meta + seq, DIFF_HEADS, HEAD_W))
        outs[2].append(s_fin)

        (ret_s,) = norm_matmul(xs, nw, w_ret, (BF16,), 512, 1024)
        (gates_s,) = norm_matmul(xs, nw, w_gate, (F32,), 512, 1024)
        (q_s,) = norm_matmul(xs, nw, w_q, (F32,), 512, 1024)
        (kv32_s,) = norm_matmul(xs, nw, w_kv, (F32,), 512, 1024)
        ret_s = jnp.pad(ret_s.reshape(db, t_new, -1), ((0, 0), (0, pad_rows - t_new), (0, 0)))
        o_ret_s, s_new = retention(log_g, ret_s, state_ret[l], pad_rows, t_new)
        q_s = q_s.reshape(db, t_new, DIFF_HEADS, 1, HEAD_W).transpose(0, 2, 3, 1, 4)
        q_rows = jnp.broadcast_to(q_s, (db, DIFF_HEADS, 2, t_new, HEAD_W)).reshape(db, -1, HEAD_W)
        kv32_s = kv32_s.reshape(db, t_new, 2 * kv_w)
        to_rows = lambda a: jnp.pad(a.reshape(db, t_new * DIFF_HEADS, HEAD_W),
                                    ((0, 0), (0, new_pad - t_new * DIFF_HEADS), (0, 0)))
        n_rows = decode_attention(page_table, lam_params, slope_rows, q_rows, cache_k, cache_v, l,
                                  to_rows(kv32_s[:, :, :kv_w]), to_rows(kv32_s[:, :, kv_w:]), lam_init, t_new)
        o_diff_s = n_rows.reshape(db, DIFF_HEADS, 2, t_new, HEAD_W)[:, :, 0].transpose(0, 2, 1, 3)
        xs = finish(xs, o_ret_s[:, :t_new].reshape(db * t_new, d), o_diff_s.reshape(db * t_new, d), gates_s, 512)
        outs[3].append(kv32_s[:, :, :kv_w].reshape(db, t_new, DIFF_HEADS, HEAD_W))
        outs[4].append(kv32_s[:, :, kv_w:].reshape(db, t_new, DIFF_HEADS, HEAD_W))
        outs[5].append(s_new)

    kp, vp, sp, ks, vs, ss = (jnp.stack(o) for o in outs)
    return (xp.reshape(b, seq, d), xs.reshape(db, t_new, d), kp, vp, sp, ks, vs, ss)
```

```python
import functools
import math

import jax
import jax.numpy as jnp
from jax import lax
from jax.experimental import pallas as pl
from jax.experimental.pallas import tpu as pltpu

F32 = jnp.float32
BF16 = jnp.bfloat16

RMS_EPS = 1e-6
N_META = 16
RET_HEADS = 4
RET_DK = 128
DIFF_HEADS = 8
DIFF_DH = 64
HEAD_W = 2 * DIFF_DH
ATT_BLOCK = 256
KV_CHUNK = 1024
SUM_ROWS = 16
DECODE_PAGES_PER_STEP = 8
VMEM_LIMIT = 56 * 1024 * 1024
NT_DIMS = (((1,), (1,)), ((), ()))
TN_DIMS = (((0,), (0,)), ((), ()))


def _params(semantics):
    return pltpu.CompilerParams(dimension_semantics=semantics, vmem_limit_bytes=VMEM_LIMIT)


def _resident(shape):
    return pl.BlockSpec(shape, lambda *_: (0,) * len(shape), pipeline_mode=pl.Buffered(1))


def _div_pow2(x, n):
    assert n & (n - 1) == 0
    return lax.shift_right_logical(x, n.bit_length() - 1)


def _mod_pow2(x, n):
    assert n & (n - 1) == 0
    return x & (n - 1)


def _rms(x, w):
    ms = jnp.mean(x * x, axis=-1, keepdims=True)
    return x * lax.rsqrt(ms + RMS_EPS) * w


def _norm_matmul_kernel(x_ref, nw_ref, w_ref, *refs):
    out_refs, h_ref = refs[:-1], refs[-1]

    @pl.when(pl.program_id(1) == 0)
    def _():
        h_ref[...] = _rms(x_ref[...], nw_ref[...]).astype(BF16)

    acc = jnp.dot(h_ref[...], w_ref[...], preferred_element_type=F32)
    for o in out_refs:
        o[...] = acc.astype(o.dtype)


def norm_matmul(x, nw, w, out_dtypes, tm, tn):
    t, k = x.shape
    n = w.shape[1]
    tm = min(tm, t)
    tn = min(tn, n)
    assert t % tm == 0 and n % tn == 0
    return pl.pallas_call(
        _norm_matmul_kernel,
        grid=(t // tm, n // tn),
        in_specs=[
            pl.BlockSpec((tm, k), lambda i, j: (i, 0)),
            pl.BlockSpec((1, k), lambda i, j: (0, 0)),
            pl.BlockSpec((k, tn), lambda i, j: (0, j)),
        ],
        out_specs=[pl.BlockSpec((tm, tn), lambda i, j: (i, j)) for _ in out_dtypes],
        out_shape=[jax.ShapeDtypeStruct((t, n), d) for d in out_dtypes],
        scratch_shapes=[pltpu.VMEM((tm, k), BF16)],
        compiler_params=_params(("parallel", "arbitrary")),
        name="norm_matmul",
    )(x, nw.reshape(1, k), w)


def _value_proj_kernel(x_ref, nw_ref, w_ref, o32_ref, vt_ref):
    h = _rms(x_ref[...], nw_ref[...]).astype(BF16)
    acc = jnp.dot(h, w_ref[...], preferred_element_type=F32)
    o32_ref[...] = acc
    heads, chunks, dv, ck = vt_ref.shape
    for hh in range(heads):
        for cc in range(chunks):
            blk = acc[cc * ck:(cc + 1) * ck, hh * dv:(hh + 1) * dv]
            vt_ref[hh, cc] = jnp.transpose(blk).astype(BF16)


def value_proj(x, nw, w, tm, ck):
    t, k = x.shape
    n = w.shape[1]
    assert t % tm == 0 and tm % ck == 0 and n == DIFF_HEADS * HEAD_W
    return pl.pallas_call(
        _value_proj_kernel,
        grid=(t // tm,),
        in_specs=[pl.BlockSpec((tm, k), lambda i: (i, 0)), _resident((1, k)), _resident((k, n))],
        out_specs=[
            pl.BlockSpec((tm, n), lambda i: (i, 0)),
            pl.BlockSpec((DIFF_HEADS, tm // ck, HEAD_W, ck), lambda i: (0, i, 0, 0)),
        ],
        out_shape=[
            jax.ShapeDtypeStruct((t, n), F32),
            jax.ShapeDtypeStruct((DIFF_HEADS, t // ck, HEAD_W, ck), BF16),
        ],
        compiler_params=_params(("parallel",)),
        name="value_proj",
    )(x, nw.reshape(1, k), w)


def _retention_kernel(logg_ref, qkv_ref, s0_ref, o_ref, sfin_ref, state_ref, *, n_valid):
    c = pl.program_id(1)
    chunk = qkv_ref.shape[1]
    heads, dk, dv = state_ref.shape

    @pl.when(c == 0)
    def _():
        state_ref[...] = s0_ref[0]

    row = lax.broadcasted_iota(jnp.int32, (chunk, chunk), 0)
    col = lax.broadcasted_iota(jnp.int32, (chunk, chunk), 1)
    rel = (row - col).astype(F32)
    idx = lax.broadcasted_iota(jnp.int32, (chunk, 1), 0).astype(F32)
    for h in range(heads):
        lg = logg_ref[h]
        q = qkv_ref[0, :, h * dk:(h + 1) * dk].astype(F32)
        k = qkv_ref[0, :, (heads + h) * dk:(heads + h + 1) * dk].astype(F32) * (dk ** -0.5)
        v = qkv_ref[0, :, 2 * heads * dk + h * dv:2 * heads * dk + (h + 1) * dv]
        decay_in = jnp.where(rel >= 0, jnp.exp(jnp.maximum(rel, 0.0) * lg), 0.0)
        q_dec = jnp.exp((idx + 1.0) * lg)
        k_dec = jnp.exp((n_valid - 1.0 - idx) * lg)
        s_dec = jnp.exp(jnp.full((1, 1), float(n_valid), F32) * lg)

        state = state_ref[h]
        scores = lax.dot_general(q.astype(BF16), k.astype(BF16), NT_DIMS, preferred_element_type=F32) * decay_in
        o = jnp.dot(scores.astype(BF16), v, preferred_element_type=F32)
        o = o + jnp.dot((q * q_dec).astype(BF16), state.astype(BF16), preferred_element_type=F32)
        o_ref[0, :, h * dv:(h + 1) * dv] = o
        state_ref[h] = s_dec * state + lax.dot_general((k * k_dec).astype(BF16), v, TN_DIMS,
                                                       preferred_element_type=F32)

    @pl.when(c == pl.num_programs(1) - 1)
    def _():
        sfin_ref[0] = state_ref[...]


def retention(log_g, qkv, s0, chunk, n_valid):
    b, t, w = qkv.shape
    dv = (w - 2 * RET_HEADS * RET_DK) // RET_HEADS
    per_batch_state = s0.shape[0] == b and b > 1
    s0_map = (lambda bi, c, lg: (bi, 0, 0, 0)) if per_batch_state else (lambda bi, c, lg: (0, 0, 0, 0))
    state_shape = (RET_HEADS, RET_DK, dv)
    grid_spec = pltpu.PrefetchScalarGridSpec(
        num_scalar_prefetch=1,
        grid=(b, t // chunk),
        in_specs=[
            pl.BlockSpec((1, chunk, w), lambda bi, c, lg: (bi, c, 0)),
            pl.BlockSpec((1,) + state_shape, s0_map),
        ],
        out_specs=[
            pl.BlockSpec((1, chunk, RET_HEADS * dv), lambda bi, c, lg: (bi, c, 0)),
            pl.BlockSpec((1,) + state_shape, lambda bi, c, lg: (bi, 0, 0, 0)),
        ],
        scratch_shapes=[pltpu.VMEM(state_shape, F32)],
    )
    return pl.pallas_call(
        functools.partial(_retention_kernel, n_valid=n_valid),
        grid_spec=grid_spec,
        out_shape=[
            jax.ShapeDtypeStruct((b, t, RET_HEADS * dv), F32),
            jax.ShapeDtypeStruct((b,) + state_shape, F32),
        ],
        compiler_params=_params(("parallel", "arbitrary")),
        name="retention",
    )(log_g, qkv, s0)


def _lambda_value(lam_ref, lam_init):
    lp = lam_ref[...]
    s1 = jnp.sum(lp[0:1] * lp[1:2], axis=-1, keepdims=True)
    s2 = jnp.sum(lp[2:3] * lp[3:4], axis=-1, keepdims=True)
    return jnp.exp(s1) - jnp.exp(s2) + lam_init


def _prompt_attn_kernel(slope_ref, lam_ref, q_ref, k_ref, vt_ref, km_ref, vtm_ref, o_ref,
                        m_ref, acc_ref, sa_ref, sb_ref, *, lam_init):
    h = pl.program_id(1)
    qi = pl.program_id(2)
    tq = q_ref.shape[1]
    ck = vt_ref.shape[3]
    n_meta_pad = km_ref.shape[0]
    slope = slope_ref[h]

    lane = lax.broadcasted_iota(jnp.int32, (tq, HEAD_W), 1)
    qs = q_ref[0].astype(F32) * (DIFF_DH ** -0.5)
    one_rows = (lax.broadcasted_iota(jnp.int32, (HEAD_W, tq), 0) < 2).astype(F32)
    qa1 = jnp.concatenate([jnp.transpose(jnp.where(lane < DIFF_DH, qs, 0.0)), one_rows], axis=0).astype(BF16)
    qa2 = jnp.concatenate([jnp.transpose(jnp.where(lane >= DIFF_DH, qs, 0.0)), one_rows], axis=0).astype(BF16)
    qa = jnp.concatenate([qa1, qa2], axis=1)

    j_row = lax.broadcasted_iota(jnp.int32, (ck, HEAD_W), 0)
    j_col = lax.broadcasted_iota(jnp.int32, (ck, HEAD_W), 1)
    j_lo = _mod_pow2(j_row, 256).astype(F32)
    j_hi = (j_row - _mod_pow2(j_row, 256)).astype(F32)
    bias_cols = (slope * jnp.where(j_col == 0, j_lo, jnp.where(j_col == 1, j_hi, 0.0))).astype(BF16)
    ones_rows = jnp.ones((SUM_ROWS, ck), BF16)

    m_ref[...] = jnp.full(m_ref.shape, -jnp.inf, F32)
    acc_ref[...] = jnp.zeros(acc_ref.shape, F32)

    def scores(k_rows):
        return jnp.dot(jnp.concatenate([k_rows, bias_cols[:k_rows.shape[0]]], axis=1), qa,
                       preferred_element_type=F32)

    def softmax_pv(s, vt, off, mask):
        shift = -slope * off.astype(F32)
        if mask is not None:
            s = jnp.where(jnp.concatenate([mask, mask], axis=1), s, -jnp.inf)
        m_old = m_ref[...]
        m_new = jnp.maximum(m_old, jnp.max(s, axis=0, keepdims=True) + shift)
        alpha = jnp.exp(m_old - m_new)
        p = jnp.exp(s - (m_new - shift)).astype(BF16)
        v_aug = jnp.concatenate([vt, ones_rows[:, :vt.shape[1]]], axis=0)
        acc_ref[...] = alpha * acc_ref[...] + jnp.dot(v_aug, p, preferred_element_type=F32)
        m_ref[...] = m_new

    def chunk_scores(c, dst_ref):
        dst_ref[...] = scores(k_ref[0, pl.ds(pl.multiple_of(c * ck, ck), ck), :])

    def chunk_update(src_ref, c, masked):
        off = qi * tq - c * ck
        mask = None
        if masked:
            key_minus_query = (lax.broadcasted_iota(jnp.int32, (ck, tq), 0)
                               - lax.broadcasted_iota(jnp.int32, (ck, tq), 1))
            mask = key_minus_query <= off
        softmax_pv(src_ref[...], vt_ref[0, c], off, mask)

    key_is_meta = lax.broadcasted_iota(jnp.int32, (n_meta_pad, tq), 0) < N_META
    softmax_pv(scores(km_ref[...]), vtm_ref[...], N_META + qi * tq, key_is_meta)

    n_full = _div_pow2(qi, ck // tq)
    n_pairs = _div_pow2(n_full, 2)
    chunk_scores(0, sa_ref)

    def chunk_pair(j, carry):
        c = 2 * j
        chunk_scores(c + 1, sb_ref)
        chunk_update(sa_ref, c, False)
        chunk_scores(c + 2, sa_ref)
        chunk_update(sb_ref, c + 1, False)
        return carry

    lax.fori_loop(0, n_pairs, chunk_pair, 0)

    @pl.when(n_full == 2 * n_pairs)
    def _():
        chunk_update(sa_ref, n_full, True)

    @pl.when(n_full != 2 * n_pairs)
    def _():
        chunk_scores(n_full, sb_ref)
        chunk_update(sa_ref, n_full - 1, False)
        chunk_update(sb_ref, n_full, True)

    lam = _lambda_value(lam_ref, lam_init)
    n = acc_ref[:HEAD_W, :] / acc_ref[HEAD_W:HEAD_W + 1, :]
    o_t = n[:, :tq] - lam * n[:, tq:]
    o_ref[0] = jnp.transpose(o_t)


def prompt_attention(slopes, lam_params, q, k, vt, k_meta, vt_meta, lam_init):
    b, t, _ = q.shape
    tq = ATT_BLOCK
    ck = vt.shape[3]
    n_meta_pad = k_meta.shape[0]
    assert t % ck == 0 and ck % tq == 0
    grid_spec = pltpu.PrefetchScalarGridSpec(
        num_scalar_prefetch=1,
        grid=(b, DIFF_HEADS, t // tq),
        in_specs=[
            pl.BlockSpec(lam_params.shape, lambda bi, h, qi, sl: (0, 0)),
            pl.BlockSpec((1, tq, HEAD_W), lambda bi, h, qi, sl: (bi, qi, h)),
            pl.BlockSpec((1, t, HEAD_W), lambda bi, h, qi, sl: (bi, 0, h)),
            pl.BlockSpec((1, t // ck, HEAD_W, ck), lambda bi, h, qi, sl: (h, bi, 0, 0)),
            pl.BlockSpec((n_meta_pad, HEAD_W), lambda bi, h, qi, sl: (0, h)),
            pl.BlockSpec((HEAD_W, n_meta_pad), lambda bi, h, qi, sl: (h, 0)),
        ],
        out_specs=pl.BlockSpec((1, tq, HEAD_W), lambda bi, h, qi, sl: (bi, qi, h)),
        scratch_shapes=[
            pltpu.VMEM((1, 2 * tq), F32), pltpu.VMEM((HEAD_W + SUM_ROWS, 2 * tq), F32),
            pltpu.VMEM((ck, 2 * tq), F32), pltpu.VMEM((ck, 2 * tq), F32),
        ],
    )
    return pl.pallas_call(
        functools.partial(_prompt_attn_kernel, lam_init=lam_init),
        grid_spec=grid_spec,
        out_shape=jax.ShapeDtypeStruct((b, t, DIFF_HEADS * HEAD_W), F32),
        compiler_params=_params(("parallel", "parallel", "arbitrary")),
        name="prompt_attn",
    )(slopes, lam_params, q, k, vt, k_meta, vt_meta)


def _decode_attn_kernel(pt_ref, lam_ref, srow_ref, q_ref, *refs, lam_init, t_new, past, n_par):
    k_refs, v_refs = refs[:n_par], refs[n_par:2 * n_par]
    kn_ref, vn_ref, o_ref, m_ref, l_ref, a_ref, b0_ref = refs[2 * n_par:]
    g = pl.program_id(1)
    rows = q_ref.shape[1]
    page, heads = k_refs[0].shape[0], k_refs[0].shape[1]
    nk = page * heads
    srow = srow_ref[:, 0:1]

    def row_parts(shape):
        r = lax.broadcasted_iota(jnp.int32, shape, 0)
        return _div_pow2(r, 2 * t_new), _mod_pow2(r, t_new)

    def col_parts(shape):
        c = lax.broadcasted_iota(jnp.int32, shape, 1)
        return _div_pow2(c, heads), _mod_pow2(c, heads)

    @pl.when(g == 0)
    def _():
        m_ref[...] = jnp.full(m_ref.shape, -jnp.inf, F32)
        l_ref[...] = jnp.zeros(l_ref.shape, F32)
        a_ref[...] = jnp.zeros(a_ref.shape, F32)
        r_head, r_tok = row_parts((rows, n_par * nk))
        c_tok, c_head = col_parts((rows, n_par * nk))
        b0_ref[...] = jnp.where(r_head == c_head, srow * (c_tok - r_tok).astype(F32), -jnp.inf)

    row = lax.broadcasted_iota(jnp.int32, (rows, HEAD_W), 0)
    lane = lax.broadcasted_iota(jnp.int32, (rows, HEAD_W), 1)
    first_map = _mod_pow2(_div_pow2(row, t_new), 2) == 0
    qa = jnp.where(first_map == (lane < DIFF_DH), q_ref[0] * (DIFF_DH ** -0.5), 0.0).astype(BF16)

    def update(t, shift, v16):
        m_old = m_ref[...]
        m_new = jnp.maximum(m_old, jnp.max(t, axis=-1, keepdims=True) + shift)
        alpha = jnp.exp(m_old - m_new)
        pr = jnp.exp(t - (m_new - shift))
        l_ref[...] = alpha * l_ref[...] + jnp.sum(pr, axis=-1, keepdims=True)
        a_ref[...] = alpha * a_ref[...] + jnp.dot(pr.astype(BF16), v16, preferred_element_type=F32)
        m_ref[...] = m_new

    k16 = jnp.concatenate([r[...].reshape(nk, HEAD_W).astype(BF16) for r in k_refs], axis=0)
    v16 = jnp.concatenate([r[...].reshape(nk, HEAD_W).astype(BF16) for r in v_refs], axis=0)
    s = lax.dot_general(qa, k16, NT_DIMS, preferred_element_type=F32)
    update(s + b0_ref[...], -srow * (past - g * (n_par * page)).astype(F32), v16)

    @pl.when(g == pl.num_programs(1) - 1)
    def _():
        n_new = kn_ref.shape[1]
        r_head, r_tok = row_parts((rows, n_new))
        c_tok, c_head = col_parts((rows, n_new))
        visible = (r_head == c_head) & (c_tok <= r_tok)
        s = lax.dot_general(qa, kn_ref[0].astype(BF16), NT_DIMS, preferred_element_type=F32)
        t = jnp.where(visible, s + srow * (c_tok - r_tok).astype(F32), -jnp.inf)
        update(t, jnp.zeros((), F32), vn_ref[0].astype(BF16))
        n = a_ref[...] / l_ref[...]
        lam = _lambda_value(lam_ref, lam_init)
        o_ref[0] = n - lam * pltpu.roll(n, rows - t_new, 0)


def decode_attention(page_table, lam_params, slope_rows, q_rows, cache_k, cache_v, layer, k_new, v_new,
                     lam_init, t_new):
    b, rows, _ = q_rows.shape
    page, heads = cache_k.shape[2], cache_k.shape[3]
    n_pages = page_table.shape[1]
    n_par = DECODE_PAGES_PER_STEP
    assert n_pages % n_par == 0

    def cache_spec(i):
        return pl.BlockSpec((None, None, page, heads, HEAD_W),
                            lambda bi, g, pt: (layer, pt[bi, g * n_par + i], 0, 0, 0))

    per_seq = lambda bi, g, pt: (bi, 0, 0)
    grid_spec = pltpu.PrefetchScalarGridSpec(
        num_scalar_prefetch=1,
        grid=(b, n_pages // n_par),
        in_specs=[
            pl.BlockSpec(lam_params.shape, lambda bi, g, pt: (0, 0)),
            pl.BlockSpec(slope_rows.shape, lambda bi, g, pt: (0, 0)),
            pl.BlockSpec((1, rows, HEAD_W), per_seq),
            *[cache_spec(i) for i in range(n_par)],
            *[cache_spec(i) for i in range(n_par)],
            pl.BlockSpec((1,) + k_new.shape[1:], per_seq),
            pl.BlockSpec((1,) + v_new.shape[1:], per_seq),
        ],
        out_specs=pl.BlockSpec((1, rows, HEAD_W), per_seq),
        scratch_shapes=[
            pltpu.VMEM((rows, 1), F32), pltpu.VMEM((rows, 1), F32), pltpu.VMEM((rows, HEAD_W), F32),
            pltpu.VMEM((rows, n_par * page * heads), F32),
        ],
    )
    return pl.pallas_call(
        functools.partial(_decode_attn_kernel, lam_init=lam_init, t_new=t_new, past=n_pages * page, n_par=n_par),
        grid_spec=grid_spec,
        out_shape=jax.ShapeDtypeStruct((b, rows, HEAD_W), F32),
        compiler_params=_params(("parallel", "arbitrary")),
        name="decode_attn",
    )(page_table, lam_params, slope_rows, q_rows, *([cache_k] * n_par), *([cache_v] * n_par), k_new, v_new)


def _merge_kernel(x_ref, oret_ref, odiff_ref, g_ref, rnw_ref, dnw_ref, wout_ref, pnw_ref, o_ref, m_ref,
                  *, lam_init):
    d = x_ref.shape[1]
    ret_dv = rnw_ref.shape[1]
    for hh in range(d // ret_dv):
        sl = slice(hh * ret_dv, (hh + 1) * ret_dv)
        rg = g_ref[:, sl]
        a = _rms(oret_ref[:, sl], rnw_ref[...]) * (rg * jax.nn.sigmoid(rg))
        m_ref[:, sl] = jax.nn.sigmoid(g_ref[:, d + hh * ret_dv:d + (hh + 1) * ret_dv]) * a
    diff_dv = dnw_ref.shape[1]
    for hh in range(d // diff_dv):
        sl = slice(hh * diff_dv, (hh + 1) * diff_dv)
        dd = _rms(odiff_ref[:, sl], dnw_ref[...]) * (1.0 - lam_init)
        gb = g_ref[:, 2 * d + hh * diff_dv:2 * d + (hh + 1) * diff_dv]
        m_ref[:, sl] = m_ref[:, sl] + jax.nn.sigmoid(gb) * dd
    mix = jnp.dot(m_ref[...].astype(BF16), wout_ref[...], preferred_element_type=F32)
    o_ref[...] = x_ref[...] + _rms(mix, pnw_ref[...])


def merge(x, o_ret, o_diff, gates, ret_nw, diff_nw, w_out, post_nw, lam_init, tm):
    t, d = x.shape
    tm = min(tm, t)
    row = lambda i: (i, 0)
    return pl.pallas_call(
        functools.partial(_merge_kernel, lam_init=lam_init),
        grid=(t // tm,),
        in_specs=[
            pl.BlockSpec((tm, d), row), pl.BlockSpec((tm, d), row), pl.BlockSpec((tm, d), row),
            pl.BlockSpec((tm, 3 * d), row),
            _resident((1, ret_nw.shape[0])), _resident((1, diff_nw.shape[0])),
            _resident((d, d)), _resident((1, d)),
        ],
        out_specs=pl.BlockSpec((tm, d), row),
        out_shape=jax.ShapeDtypeStruct((t, d), F32),
        scratch_shapes=[pltpu.VMEM((tm, d), F32)],
        compiler_params=_params(("parallel",)),
        name="merge",
    )(x, o_ret, o_diff, gates, ret_nw.reshape(1, -1), diff_nw.reshape(1, -1), w_out, post_nw.reshape(1, -1))


def _ffn_kernel(x_ref, pre_ref, wg_ref, wu_ref, wd_ref, post_ref, o_ref, *, n_split):
    x = x_ref[...]
    h = _rms(x, pre_ref[...]).astype(BF16)
    dff = wg_ref.shape[1]
    step = dff // n_split
    y = jnp.zeros(x.shape, F32)
    for s in range(n_split):
        sl = slice(s * step, (s + 1) * step)
        g = jnp.dot(h, wg_ref[:, sl], preferred_element_type=F32)
        u = jnp.dot(h, wu_ref[:, sl], preferred_element_type=F32)
        act = (g * jax.nn.sigmoid(g) * u).astype(BF16)
        y = y + jnp.dot(act, wd_ref[sl, :], preferred_element_type=F32)
    o_ref[...] = x + _rms(y, post_ref[...])


def ffn(x, pre_nw, w_gate, w_up, w_down, post_nw, tm, n_split):
    t, d = x.shape
    dff = w_gate.shape[1]
    tm = min(tm, t)
    row = lambda i: (i, 0)
    return pl.pallas_call(
        functools.partial(_ffn_kernel, n_split=n_split),
        grid=(t // tm,),
        in_specs=[
            pl.BlockSpec((tm, d), row), _resident((1, d)),
            _resident((d, dff)), _resident((d, dff)), _resident((dff, d)),
            _resident((1, d)),
        ],
        out_specs=pl.BlockSpec((tm, d), row),
        out_shape=jax.ShapeDtypeStruct((t, d), F32),
        compiler_params=_params(("parallel",)),
        name="ffn",
    )(x, pre_nw.reshape(1, -1), w_gate, w_up, w_down, post_nw.reshape(1, -1))


def kernel(x_prompt, x_sample, cache_k, cache_v, state_ret, page_table, meta_tokens, norm_mix_pre,
           norm_mix_post, w_in, ret_norm, lambda_q1, lambda_k1, lambda_q2, lambda_k2, diff_norm, w_out,
           norm_ffn_pre, norm_ffn_post, w_ffn_gate, w_ffn_up, w_ffn_down):
    b, seq, d = x_prompt.shape
    db, t_new, _ = x_sample.shape
    depth = cache_k.shape[0]
    ret_qk = RET_HEADS * RET_DK
    kv_w = DIFF_HEADS * HEAD_W
    n_meta = meta_tokens.shape[0]
    assert n_meta == N_META
    log_g = jnp.log1p(-jnp.exp2(-5.0 - jnp.arange(RET_HEADS, dtype=F32)))
    slopes = jnp.exp2(-8.0 * (jnp.arange(DIFF_HEADS, dtype=F32) + 1.0) / DIFF_HEADS)
    pad_rows = 16
    meta_pad = 128
    new_pad = 128
    slope_rows = jnp.broadcast_to(jnp.repeat(slopes, 2 * t_new)[:, None], (DIFF_HEADS * 2 * t_new, 128))

    xp = x_prompt.reshape(b * seq, d)
    xs = x_sample.reshape(db * t_new, d)
    outs = [[] for _ in range(6)]
    for l in range(depth):
        lam_init = 0.8 - 0.6 * math.exp(-0.3 * l)
        lam_params = jnp.stack([lambda_q1[l], lambda_k1[l], lambda_q2[l], lambda_k2[l]]).astype(F32)
        wl = w_in[l].astype(BF16)
        o0 = 2 * ret_qk + 2 * d
        w_ret = wl[:, :2 * ret_qk + d]
        w_gate = jnp.concatenate([wl[:, 2 * ret_qk + d:o0], wl[:, o0 + 3 * kv_w:]], axis=1)
        w_q = wl[:, o0:o0 + kv_w]
        w_k = wl[:, o0 + kv_w:o0 + 2 * kv_w]
        w_v = wl[:, o0 + 2 * kv_w:o0 + 3 * kv_w]
        w_kv = wl[:, o0 + kv_w:o0 + 3 * kv_w]
        w_o = w_out[l].astype(BF16)
        w_g, w_u, w_d = (w.astype(BF16) for w in (w_ffn_gate[l], w_ffn_up[l], w_ffn_down[l]))
        nw = norm_mix_pre[l]

        def finish(x, o_ret, o_diff, gates, tm):
            x = merge(x, o_ret, o_diff, gates, ret_norm[l], diff_norm[l], w_o, norm_mix_post[l], lam_init, tm)
            return ffn(x, norm_ffn_pre[l], w_g, w_u, w_d, norm_ffn_post[l], tm, 2)

        xm = meta_tokens.astype(F32)
        (ret_m,) = norm_matmul(xm, nw, w_ret, (BF16,), n_meta, 1024)
        (kv32_m,) = norm_matmul(xm, nw, w_kv, (F32,), n_meta, 1024)
        zero_state = jnp.zeros((1, RET_HEADS, RET_DK, d // RET_HEADS), F32)
        _, s_meta = retention(log_g, ret_m[None], zero_state, n_meta, n_meta)
        kv16_m = jnp.pad(kv32_m.astype(BF16), ((0, meta_pad - n_meta), (0, 0)))
        k_meta, vt_meta = kv16_m[:, :kv_w], kv16_m[:, kv_w:].T

        (ret_p,) = norm_matmul(xp, nw, w_ret, (BF16,), 1024, 1024)
        (gates_p,) = norm_matmul(xp, nw, w_gate, (F32,), 1024, 1024)
        (q_p,) = norm_matmul(xp, nw, w_q, (BF16,), 1024, 1024)
        k32_p, k16_p = norm_matmul(xp, nw, w_k, (F32, BF16), 1024, 1024)
        v32_p, vt_p = value_proj(xp, nw, w_v, min(KV_CHUNK, seq), min(KV_CHUNK, seq))
        o_ret, s_fin = retention(log_g, ret_p.reshape(b, seq, -1), s_meta, 128, 128)
        o_diff = prompt_attention(slopes, lam_params, q_p.reshape(b, seq, kv_w), k16_p.reshape(b, seq, kv_w),
                                  vt_p, k_meta, vt_meta, lam_init)
        xp = finish(xp, o_ret.reshape(b * seq, d), o_diff.reshape(b * seq, d), gates_p, 512)
        for o, main, lo in ((outs[0], k32_p, 0), (outs[1], v32_p, kv_w)):
            meta_part = jnp.broadcast_to(kv32_m[None, :, lo:lo + kv_w], (b, n_meta, kv_w))
            both = jnp.concatenate([meta_part, main.reshape(b, seq, kv_w)], axis=1)
            o.append(both.reshape(b, n_meta + seq, DIFF_HEADS, HEAD_W))
        outs[2].append(s_fin)

        (ret_s,) = norm_matmul(xs, nw, w_ret, (BF16,), 512, 1024)
        (gates_s,) = norm_matmul(xs, nw, w_gate, (F32,), 512, 1024)
        (q_s,) = norm_matmul(xs, nw, w_q, (F32,), 512, 1024)
        (kv32_s,) = norm_matmul(xs, nw, w_kv, (F32,), 512, 1024)
        ret_s = jnp.pad(ret_s.reshape(db, t_new, -1), ((0, 0), (0, pad_rows - t_new), (0, 0)))
        o_ret_s, s_new = retention(log_g, ret_s, state_ret[l], pad_rows, t_new)
        q_s = q_s.reshape(db, t_new, DIFF_HEADS, 1, HEAD_W).transpose(0, 2, 3, 1, 4)
        q_rows = jnp.broadcast_to(q_s, (db, DIFF_HEADS, 2, t_new, HEAD_W)).reshape(db, -1, HEAD_W)
        kv32_s = kv32_s.reshape(db, t_new, 2 * kv_w)
        to_rows = lambda a: jnp.pad(a.reshape(db, t_new * DIFF_HEADS, HEAD_W),
                                    ((0, 0), (0, new_pad - t_new * DIFF_HEADS), (0, 0)))
        n_rows = decode_attention(page_table, lam_params, slope_rows, q_rows, cache_k, cache_v, l,
                                  to_rows(kv32_s[:, :, :kv_w]), to_rows(kv32_s[:, :, kv_w:]), lam_init, t_new)
        o_diff_s = n_rows.reshape(db, DIFF_HEADS, 2, t_new, HEAD_W)[:, :, 0].transpose(0, 2, 1, 3)
        xs = finish(xs, o_ret_s[:, :t_new].reshape(db * t_new, d), o_diff_s.reshape(db * t_new, d), gates_s, 512)
        outs[3].append(kv32_s[:, :, :kv_w].reshape(db, t_new, DIFF_HEADS, HEAD_W))
        outs[4].append(kv32_s[:, :, kv_w:].reshape(db, t_new, DIFF_HEADS, HEAD_W))
        outs[5].append(s_new)

    kp, vp, sp, ks, vs, ss = (jnp.stack(o) for o in outs)
    return (xp.reshape(b, seq, d), xs.reshape(db, t_new, d), kp, vp, sp, ks, vs, ss)
```

```python
import functools
import math

import jax
import jax.numpy as jnp
from jax import lax
from jax.experimental import pallas as pl
from jax.experimental.pallas import tpu as pltpu

F32 = jnp.float32
BF16 = jnp.bfloat16

RMS_EPS = 1e-6
N_META = 16
RET_HEADS = 4
RET_DK = 128
DIFF_HEADS = 8
DIFF_DH = 64
HEAD_W = 2 * DIFF_DH
ATT_BLOCK = 512
KV_CHUNK = 1024
SUM_ROWS = 16
DECODE_PAGES_PER_STEP = 8
VMEM_LIMIT = 56 * 1024 * 1024
NT_DIMS = (((1,), (1,)), ((), ()))
TN_DIMS = (((0,), (0,)), ((), ()))


def _params(semantics):
    return pltpu.CompilerParams(dimension_semantics=semantics, vmem_limit_bytes=VMEM_LIMIT)


def _resident(shape):
    return pl.BlockSpec(shape, lambda *_: (0,) * len(shape), pipeline_mode=pl.Buffered(1))


def _div_pow2(x, n):
    assert n & (n - 1) == 0
    return lax.shift_right_logical(x, n.bit_length() - 1)


def _mod_pow2(x, n):
    assert n & (n - 1) == 0
    return x & (n - 1)


def _rms(x, w):
    ms = jnp.mean(x * x, axis=-1, keepdims=True)
    return x * lax.rsqrt(ms + RMS_EPS) * w


def _norm_matmul_kernel(x_ref, nw_ref, w_ref, *refs):
    out_refs, h_ref = refs[:-1], refs[-1]

    @pl.when(pl.program_id(1) == 0)
    def _():
        h_ref[...] = _rms(x_ref[...], nw_ref[...]).astype(BF16)

    acc = jnp.dot(h_ref[...], w_ref[...], preferred_element_type=F32)
    for o in out_refs:
        o[...] = acc.astype(o.dtype)


def norm_matmul(x, nw, w, out_dtypes, tm, tn):
    t, k = x.shape
    n = w.shape[1]
    tm = min(tm, t)
    tn = min(tn, n)
    assert t % tm == 0 and n % tn == 0
    return pl.pallas_call(
        _norm_matmul_kernel,
        grid=(t // tm, n // tn),
        in_specs=[
            pl.BlockSpec((tm, k), lambda i, j: (i, 0)),
            pl.BlockSpec((1, k), lambda i, j: (0, 0)),
            pl.BlockSpec((k, tn), lambda i, j: (0, j)),
        ],
        out_specs=[pl.BlockSpec((tm, tn), lambda i, j: (i, j)) for _ in out_dtypes],
        out_shape=[jax.ShapeDtypeStruct((t, n), d) for d in out_dtypes],
        scratch_shapes=[pltpu.VMEM((tm, k), BF16)],
        compiler_params=_params(("parallel", "arbitrary")),
        name="norm_matmul",
    )(x, nw.reshape(1, k), w)


def _value_proj_kernel(x_ref, nw_ref, w_ref, o32_ref, vt_ref):
    h = _rms(x_ref[...], nw_ref[...]).astype(BF16)
    acc = jnp.dot(h, w_ref[...], preferred_element_type=F32)
    o32_ref[...] = acc
    heads, chunks, dv, ck = vt_ref.shape
    for hh in range(heads):
        for cc in range(chunks):
            blk = acc[cc * ck:(cc + 1) * ck, hh * dv:(hh + 1) * dv]
            vt_ref[hh, cc] = jnp.transpose(blk).astype(BF16)


def value_proj(x, nw, w, tm, ck):
    t, k = x.shape
    n = w.shape[1]
    assert t % tm == 0 and tm % ck == 0 and n == DIFF_HEADS * HEAD_W
    return pl.pallas_call(
        _value_proj_kernel,
        grid=(t // tm,),
        in_specs=[pl.BlockSpec((tm, k), lambda i: (i, 0)), _resident((1, k)), _resident((k, n))],
        out_specs=[
            pl.BlockSpec((tm, n), lambda i: (i, 0)),
            pl.BlockSpec((DIFF_HEADS, tm // ck, HEAD_W, ck), lambda i: (0, i, 0, 0)),
        ],
        out_shape=[
            jax.ShapeDtypeStruct((t, n), F32),
            jax.ShapeDtypeStruct((DIFF_HEADS, t // ck, HEAD_W, ck), BF16),
        ],
        compiler_params=_params(("parallel",)),
        name="value_proj",
    )(x, nw.reshape(1, k), w)


def _retention_kernel(logg_ref, qkv_ref, s0_ref, o_ref, sfin_ref, state_ref, *, n_valid):
    c = pl.program_id(1)
    chunk = qkv_ref.shape[1]
    heads, dk, dv = state_ref.shape

    @pl.when(c == 0)
    def _():
        state_ref[...] = s0_ref[0]

    row = lax.broadcasted_iota(jnp.int32, (chunk, chunk), 0)
    col = lax.broadcasted_iota(jnp.int32, (chunk, chunk), 1)
    rel = (row - col).astype(F32)
    idx = lax.broadcasted_iota(jnp.int32, (chunk, 1), 0).astype(F32)
    for h in range(heads):
        lg = logg_ref[h]
        q = qkv_ref[0, :, h * dk:(h + 1) * dk].astype(F32)
        k = qkv_ref[0, :, (heads + h) * dk:(heads + h + 1) * dk].astype(F32) * (dk ** -0.5)
        v = qkv_ref[0, :, 2 * heads * dk + h * dv:2 * heads * dk + (h + 1) * dv]
        decay_in = jnp.where(rel >= 0, jnp.exp(jnp.maximum(rel, 0.0) * lg), 0.0)
        q_dec = jnp.exp((idx + 1.0) * lg)
        k_dec = jnp.exp((n_valid - 1.0 - idx) * lg)
        s_dec = jnp.exp(jnp.full((1, 1), float(n_valid), F32) * lg)

        state = state_ref[h]
        scores = lax.dot_general(q.astype(BF16), k.astype(BF16), NT_DIMS, preferred_element_type=F32) * decay_in
        o = jnp.dot(scores.astype(BF16), v, preferred_element_type=F32)
        o = o + jnp.dot((q * q_dec).astype(BF16), state.astype(BF16), preferred_element_type=F32)
        o_ref[0, :, h * dv:(h + 1) * dv] = o.astype(o_ref.dtype)
        state_ref[h] = s_dec * state + lax.dot_general((k * k_dec).astype(BF16), v, TN_DIMS,
                                                       preferred_element_type=F32)

    @pl.when(c == pl.num_programs(1) - 1)
    def _():
        sfin_ref[0] = state_ref[...]


def retention(log_g, qkv, s0, chunk, n_valid):
    b, t, w = qkv.shape
    dv = (w - 2 * RET_HEADS * RET_DK) // RET_HEADS
    per_batch_state = s0.shape[0] == b and b > 1
    s0_map = (lambda bi, c, lg: (bi, 0, 0, 0)) if per_batch_state else (lambda bi, c, lg: (0, 0, 0, 0))
    state_shape = (RET_HEADS, RET_DK, dv)
    grid_spec = pltpu.PrefetchScalarGridSpec(
        num_scalar_prefetch=1,
        grid=(b, t // chunk),
        in_specs=[
            pl.BlockSpec((1, chunk, w), lambda bi, c, lg: (bi, c, 0)),
            pl.BlockSpec((1,) + state_shape, s0_map),
        ],
        out_specs=[
            pl.BlockSpec((1, chunk, RET_HEADS * dv), lambda bi, c, lg: (bi, c, 0)),
            pl.BlockSpec((1,) + state_shape, lambda bi, c, lg: (bi, 0, 0, 0)),
        ],
        scratch_shapes=[pltpu.VMEM(state_shape, F32)],
    )
    return pl.pallas_call(
        functools.partial(_retention_kernel, n_valid=n_valid),
        grid_spec=grid_spec,
        out_shape=[
            jax.ShapeDtypeStruct((b, t, RET_HEADS * dv), BF16),
            jax.ShapeDtypeStruct((b,) + state_shape, F32),
        ],
        compiler_params=_params(("parallel", "arbitrary")),
        name="retention",
    )(log_g, qkv, s0)


def _lambda_value(lam_ref, lam_init):
    lp = lam_ref[...]
    s1 = jnp.sum(lp[0:1] * lp[1:2], axis=-1, keepdims=True)
    s2 = jnp.sum(lp[2:3] * lp[3:4], axis=-1, keepdims=True)
    return jnp.exp(s1) - jnp.exp(s2) + lam_init


def _prompt_attn_kernel(slope_ref, lam_ref, q_ref, k_ref, vt_ref, km_ref, vtm_ref, o_ref,
                        m_ref, acc_ref, sa_ref, sb_ref, *, lam_init):
    h = pl.program_id(1)
    qi = pl.program_id(2)
    tq = q_ref.shape[1]
    ck = vt_ref.shape[3]
    n_meta_pad = km_ref.shape[0]
    slope = slope_ref[h]

    lane = lax.broadcasted_iota(jnp.int32, (tq, HEAD_W), 1)
    qs = q_ref[0].astype(F32) * (DIFF_DH ** -0.5)
    one_rows = (lax.broadcasted_iota(jnp.int32, (HEAD_W, tq), 0) < 2).astype(F32)
    qa1 = jnp.concatenate([jnp.transpose(jnp.where(lane < DIFF_DH, qs, 0.0)), one_rows], axis=0).astype(BF16)
    qa2 = jnp.concatenate([jnp.transpose(jnp.where(lane >= DIFF_DH, qs, 0.0)), one_rows], axis=0).astype(BF16)
    qa = jnp.concatenate([qa1, qa2], axis=1)

    j_row = lax.broadcasted_iota(jnp.int32, (ck, HEAD_W), 0)
    j_col = lax.broadcasted_iota(jnp.int32, (ck, HEAD_W), 1)
    j_lo = _mod_pow2(j_row, 256).astype(F32)
    j_hi = (j_row - _mod_pow2(j_row, 256)).astype(F32)
    bias_cols = (slope * jnp.where(j_col == 0, j_lo, jnp.where(j_col == 1, j_hi, 0.0))).astype(BF16)
    ones_rows = jnp.ones((SUM_ROWS, ck), BF16)

    m_ref[...] = jnp.full(m_ref.shape, -jnp.inf, F32)
    acc_ref[...] = jnp.zeros(acc_ref.shape, F32)

    def scores(k_rows):
        return jnp.dot(jnp.concatenate([k_rows, bias_cols[:k_rows.shape[0]]], axis=1), qa,
                       preferred_element_type=F32)

    def softmax_pv(s, vt, off, mask):
        shift = -slope * jnp.asarray(off, jnp.int32).astype(F32)
        if mask is not None:
            s = jnp.where(jnp.concatenate([mask, mask], axis=1), s, -jnp.inf)
        m_old = m_ref[...]
        m_new = jnp.maximum(m_old, jnp.max(s, axis=0, keepdims=True) + shift)
        alpha = jnp.exp(m_old - m_new)
        p = jnp.exp(s - (m_new - shift)).astype(BF16)
        v_aug = jnp.concatenate([vt, ones_rows[:, :vt.shape[1]]], axis=0)
        acc_ref[...] = alpha * acc_ref[...] + jnp.dot(v_aug, p, preferred_element_type=F32)
        m_ref[...] = m_new

    def chunk_scores(c, dst_ref):
        dst_ref[...] = scores(k_ref[0, pl.ds(pl.multiple_of(c * ck, ck), ck), :])

    def chunk_update(src_ref, c):
        softmax_pv(src_ref[...], vt_ref[0, c], qi * tq - c * ck, None)

    def causal_chunk_update(src_ref, c):
        v_dyn = qi - c * (ck // tq)
        for v in range(ck // tq):
            @pl.when(v_dyn == v)
            def _():
                kk = (v + 1) * tq
                key_minus_query = (lax.broadcasted_iota(jnp.int32, (kk, tq), 0)
                                   - lax.broadcasted_iota(jnp.int32, (kk, tq), 1))
                softmax_pv(src_ref[:kk, :], vt_ref[0, c, :, :kk], v * tq, key_minus_query <= v * tq)

    key_is_meta = lax.broadcasted_iota(jnp.int32, (n_meta_pad, tq), 0) < N_META
    softmax_pv(scores(km_ref[...]), vtm_ref[...], N_META + qi * tq, key_is_meta)

    n_full = _div_pow2(qi, ck // tq)
    n_pairs = _div_pow2(n_full, 2)
    chunk_scores(0, sa_ref)

    def chunk_pair(j, carry):
        c = 2 * j
        chunk_scores(c + 1, sb_ref)
        chunk_update(sa_ref, c)
        chunk_scores(c + 2, sa_ref)
        chunk_update(sb_ref, c + 1)
        return carry

    lax.fori_loop(0, n_pairs, chunk_pair, 0)

    @pl.when(n_full == 2 * n_pairs)
    def _():
        causal_chunk_update(sa_ref, n_full)

    @pl.when(n_full != 2 * n_pairs)
    def _():
        chunk_scores(n_full, sb_ref)
        chunk_update(sa_ref, n_full - 1)
        causal_chunk_update(sb_ref, n_full)

    lam = _lambda_value(lam_ref, lam_init)
    n = acc_ref[:HEAD_W, :] / acc_ref[HEAD_W:HEAD_W + 1, :]
    o_t = n[:, :tq] - lam * n[:, tq:]
    o_ref[0] = jnp.transpose(o_t).astype(o_ref.dtype)


def prompt_attention(slopes, lam_params, q, k, vt, k_meta, vt_meta, lam_init):
    b, t, _ = q.shape
    tq = ATT_BLOCK
    ck = vt.shape[3]
    n_meta_pad = k_meta.shape[0]
    assert t % ck == 0 and ck % tq == 0
    grid_spec = pltpu.PrefetchScalarGridSpec(
        num_scalar_prefetch=1,
        grid=(b, DIFF_HEADS, t // tq),
        in_specs=[
            pl.BlockSpec(lam_params.shape, lambda bi, h, qi, sl: (0, 0)),
            pl.BlockSpec((1, tq, HEAD_W), lambda bi, h, qi, sl: (bi, qi, h)),
            pl.BlockSpec((1, t, HEAD_W), lambda bi, h, qi, sl: (bi, 0, h)),
            pl.BlockSpec((1, t // ck, HEAD_W, ck), lambda bi, h, qi, sl: (h, bi, 0, 0)),
            pl.BlockSpec((n_meta_pad, HEAD_W), lambda bi, h, qi, sl: (0, h)),
            pl.BlockSpec((HEAD_W, n_meta_pad), lambda bi, h, qi, sl: (h, 0)),
        ],
        out_specs=pl.BlockSpec((1, tq, HEAD_W), lambda bi, h, qi, sl: (bi, qi, h)),
        scratch_shapes=[
            pltpu.VMEM((1, 2 * tq), F32), pltpu.VMEM((HEAD_W + SUM_ROWS, 2 * tq), F32),
            pltpu.VMEM((ck, 2 * tq), F32), pltpu.VMEM((ck, 2 * tq), F32),
        ],
    )
    return pl.pallas_call(
        functools.partial(_prompt_attn_kernel, lam_init=lam_init),
        grid_spec=grid_spec,
        out_shape=jax.ShapeDtypeStruct((b, t, DIFF_HEADS * HEAD_W), BF16),
        compiler_params=_params(("parallel", "parallel", "arbitrary")),
        name="prompt_attn",
    )(slopes, lam_params, q, k, vt, k_meta, vt_meta)


def _decode_attn_kernel(pt_ref, lam_ref, srow_ref, q_ref, *refs, lam_init, t_new, past, n_par):
    k_refs, v_refs = refs[:n_par], refs[n_par:2 * n_par]
    kn_ref, vn_ref, o_ref, m_ref, l_ref, a_ref, b0_ref = refs[2 * n_par:]
    g = pl.program_id(1)
    rows = q_ref.shape[1]
    page, heads = k_refs[0].shape[0], k_refs[0].shape[1]
    nk = page * heads
    srow = srow_ref[:, 0:1]

    def row_parts(shape):
        r = lax.broadcasted_iota(jnp.int32, shape, 0)
        return _div_pow2(r, 2 * t_new), _mod_pow2(r, t_new)

    def col_parts(shape):
        c = lax.broadcasted_iota(jnp.int32, shape, 1)
        return _div_pow2(c, heads), _mod_pow2(c, heads)

    @pl.when(g == 0)
    def _():
        m_ref[...] = jnp.full(m_ref.shape, -jnp.inf, F32)
        l_ref[...] = jnp.zeros(l_ref.shape, F32)
        a_ref[...] = jnp.zeros(a_ref.shape, F32)
        r_head, r_tok = row_parts((rows, n_par * nk))
        c_tok, c_head = col_parts((rows, n_par * nk))
        b0_ref[...] = jnp.where(r_head == c_head, srow * (c_tok - r_tok).astype(F32), -jnp.inf)

    row = lax.broadcasted_iota(jnp.int32, (rows, HEAD_W), 0)
    lane = lax.broadcasted_iota(jnp.int32, (rows, HEAD_W), 1)
    first_map = _mod_pow2(_div_pow2(row, t_new), 2) == 0
    qa = jnp.where(first_map == (lane < DIFF_DH), q_ref[0] * (DIFF_DH ** -0.5), 0.0).astype(BF16)

    def update(t, shift, v16):
        m_old = m_ref[...]
        m_new = jnp.maximum(m_old, jnp.max(t, axis=-1, keepdims=True) + shift)
        alpha = jnp.exp(m_old - m_new)
        pr = jnp.exp(t - (m_new - shift))
        l_ref[...] = alpha * l_ref[...] + jnp.sum(pr, axis=-1, keepdims=True)
        a_ref[...] = alpha * a_ref[...] + jnp.dot(pr.astype(BF16), v16, preferred_element_type=F32)
        m_ref[...] = m_new

    k16 = jnp.concatenate([r[...].reshape(nk, HEAD_W).astype(BF16) for r in k_refs], axis=0)
    v16 = jnp.concatenate([r[...].reshape(nk, HEAD_W).astype(BF16) for r in v_refs], axis=0)
    s = lax.dot_general(qa, k16, NT_DIMS, preferred_element_type=F32)
    update(s + b0_ref[...], -srow * (past - g * (n_par * page)).astype(F32), v16)

    @pl.when(g == pl.num_programs(1) - 1)
    def _():
        n_new = kn_ref.shape[1]
        r_head, r_tok = row_parts((rows, n_new))
        c_tok, c_head = col_parts((rows, n_new))
        visible = (r_head == c_head) & (c_tok <= r_tok)
        s = lax.dot_general(qa, kn_ref[0].astype(BF16), NT_DIMS, preferred_element_type=F32)
        t = jnp.where(visible, s + srow * (c_tok - r_tok).astype(F32), -jnp.inf)
        update(t, jnp.zeros((), F32), vn_ref[0].astype(BF16))
        n = a_ref[...] / l_ref[...]
        lam = _lambda_value(lam_ref, lam_init)
        o_ref[0] = n - lam * pltpu.roll(n, rows - t_new, 0)


def decode_attention(page_table, lam_params, slope_rows, q_rows, cache_k, cache_v, layer, k_new, v_new,
                     lam_init, t_new):
    b, rows, _ = q_rows.shape
    page, heads = cache_k.shape[2], cache_k.shape[3]
    n_pages = page_table.shape[1]
    n_par = DECODE_PAGES_PER_STEP
    assert n_pages % n_par == 0

    def cache_spec(i):
        return pl.BlockSpec((None, None, page, heads, HEAD_W),
                            lambda bi, g, pt: (layer, pt[bi, g * n_par + i], 0, 0, 0))

    per_seq = lambda bi, g, pt: (bi, 0, 0)
    grid_spec = pltpu.PrefetchScalarGridSpec(
        num_scalar_prefetch=1,
        grid=(b, n_pages // n_par),
        in_specs=[
            pl.BlockSpec(lam_params.shape, lambda bi, g, pt: (0, 0)),
            pl.BlockSpec(slope_rows.shape, lambda bi, g, pt: (0, 0)),
            pl.BlockSpec((1, rows, HEAD_W), per_seq),
            *[cache_spec(i) for i in range(n_par)],
            *[cache_spec(i) for i in range(n_par)],
            pl.BlockSpec((1,) + k_new.shape[1:], per_seq),
            pl.BlockSpec((1,) + v_new.shape[1:], per_seq),
        ],
        out_specs=pl.BlockSpec((1, rows, HEAD_W), per_seq),
        scratch_shapes=[
            pltpu.VMEM((rows, 1), F32), pltpu.VMEM((rows, 1), F32), pltpu.VMEM((rows, HEAD_W), F32),
            pltpu.VMEM((rows, n_par * page * heads), F32),
        ],
    )
    return pl.pallas_call(
        functools.partial(_decode_attn_kernel, lam_init=lam_init, t_new=t_new, past=n_pages * page, n_par=n_par),
        grid_spec=grid_spec,
        out_shape=jax.ShapeDtypeStruct((b, rows, HEAD_W), F32),
        compiler_params=_params(("parallel", "arbitrary")),
        name="decode_attn",
    )(page_table, lam_params, slope_rows, q_rows, *([cache_k] * n_par), *([cache_v] * n_par), k_new, v_new)


def _merge_kernel(x_ref, oret_ref, odiff_ref, g_ref, rnw_ref, dnw_ref, wout_ref, pnw_ref, o_ref, m_ref,
                  *, lam_init):
    d = x_ref.shape[1]
    ret_dv = rnw_ref.shape[1]
    for hh in range(d // ret_dv):
        sl = slice(hh * ret_dv, (hh + 1) * ret_dv)
        rg = g_ref[:, sl].astype(F32)
        a = _rms(oret_ref[:, sl].astype(F32), rnw_ref[...]) * (rg * jax.nn.sigmoid(rg))
        ga = g_ref[:, d + hh * ret_dv:d + (hh + 1) * ret_dv].astype(F32)
        m_ref[:, sl] = jax.nn.sigmoid(ga) * a
    diff_dv = dnw_ref.shape[1]
    for hh in range(d // diff_dv):
        sl = slice(hh * diff_dv, (hh + 1) * diff_dv)
        dd = _rms(odiff_ref[:, sl].astype(F32), dnw_ref[...]) * (1.0 - lam_init)
        gb = g_ref[:, 2 * d + hh * diff_dv:2 * d + (hh + 1) * diff_dv].astype(F32)
        m_ref[:, sl] = m_ref[:, sl] + jax.nn.sigmoid(gb) * dd
    mix = jnp.dot(m_ref[...].astype(BF16), wout_ref[...], preferred_element_type=F32)
    o_ref[...] = x_ref[...] + _rms(mix, pnw_ref[...])


def merge(x, o_ret, o_diff, gates, ret_nw, diff_nw, w_out, post_nw, lam_init, tm):
    t, d = x.shape
    tm = min(tm, t)
    row = lambda i: (i, 0)
    return pl.pallas_call(
        functools.partial(_merge_kernel, lam_init=lam_init),
        grid=(t // tm,),
        in_specs=[
            pl.BlockSpec((tm, d), row), pl.BlockSpec((tm, d), row), pl.BlockSpec((tm, d), row),
            pl.BlockSpec((tm, 3 * d), row),
            _resident((1, ret_nw.shape[0])), _resident((1, diff_nw.shape[0])),
            _resident((d, d)), _resident((1, d)),
        ],
        out_specs=pl.BlockSpec((tm, d), row),
        out_shape=jax.ShapeDtypeStruct((t, d), F32),
        scratch_shapes=[pltpu.VMEM((tm, d), F32)],
        compiler_params=_params(("parallel",)),
        name="merge",
    )(x, o_ret, o_diff, gates, ret_nw.reshape(1, -1), diff_nw.reshape(1, -1), w_out, post_nw.reshape(1, -1))


def _ffn_kernel(x_ref, pre_ref, wg_ref, wu_ref, wd_ref, post_ref, o_ref, *, n_split):
    x = x_ref[...]
    h = _rms(x, pre_ref[...]).astype(BF16)
    dff = wg_ref.shape[1]
    step = dff // n_split
    y = jnp.zeros(x.shape, F32)
    for s in range(n_split):
        sl = slice(s * step, (s + 1) * step)
        g = jnp.dot(h, wg_ref[:, sl], preferred_element_type=F32)
        u = jnp.dot(h, wu_ref[:, sl], preferred_element_type=F32)
        act = (g * jax.nn.sigmoid(g) * u).astype(BF16)
        y = y + jnp.dot(act, wd_ref[sl, :], preferred_element_type=F32)
    o_ref[...] = x + _rms(y, post_ref[...])


def ffn(x, pre_nw, w_gate, w_up, w_down, post_nw, tm, n_split):
    t, d = x.shape
    dff = w_gate.shape[1]
    tm = min(tm, t)
    row = lambda i: (i, 0)
    return pl.pallas_call(
        functools.partial(_ffn_kernel, n_split=n_split),
        grid=(t // tm,),
        in_specs=[
            pl.BlockSpec((tm, d), row), _resident((1, d)),
            _resident((d, dff)), _resident((d, dff)), _resident((dff, d)),
            _resident((1, d)),
        ],
        out_specs=pl.BlockSpec((tm, d), row),
        out_shape=jax.ShapeDtypeStruct((t, d), F32),
        compiler_params=_params(("parallel",)),
        name="ffn",
    )(x, pre_nw.reshape(1, -1), w_gate, w_up, w_down, post_nw.reshape(1, -1))


def kernel(x_prompt, x_sample, cache_k, cache_v, state_ret, page_table, meta_tokens, norm_mix_pre,
           norm_mix_post, w_in, ret_norm, lambda_q1, lambda_k1, lambda_q2, lambda_k2, diff_norm, w_out,
           norm_ffn_pre, norm_ffn_post, w_ffn_gate, w_ffn_up, w_ffn_down):
    b, seq, d = x_prompt.shape
    db, t_new, _ = x_sample.shape
    depth = cache_k.shape[0]
    ret_qk = RET_HEADS * RET_DK
    kv_w = DIFF_HEADS * HEAD_W
    n_meta = meta_tokens.shape[0]
    assert n_meta == N_META
    log_g = jnp.log1p(-jnp.exp2(-5.0 - jnp.arange(RET_HEADS, dtype=F32)))
    slopes = jnp.exp2(-8.0 * (jnp.arange(DIFF_HEADS, dtype=F32) + 1.0) / DIFF_HEADS)
    pad_rows = 16
    meta_pad = 128
    new_pad = 128
    slope_rows = jnp.broadcast_to(jnp.repeat(slopes, 2 * t_new)[:, None], (DIFF_HEADS * 2 * t_new, 128))

    xp = x_prompt.reshape(b * seq, d)
    xs = x_sample.reshape(db * t_new, d)
    outs = [[] for _ in range(6)]
    for l in range(depth):
        lam_init = 0.8 - 0.6 * math.exp(-0.3 * l)
        lam_params = jnp.stack([lambda_q1[l], lambda_k1[l], lambda_q2[l], lambda_k2[l]]).astype(F32)
        wl = w_in[l].astype(BF16)
        o0 = 2 * ret_qk + 2 * d
        w_ret = wl[:, :2 * ret_qk + d]
        w_gate = jnp.concatenate([wl[:, 2 * ret_qk + d:o0], wl[:, o0 + 3 * kv_w:]], axis=1)
        w_q = wl[:, o0:o0 + kv_w]
        w_k = wl[:, o0 + kv_w:o0 + 2 * kv_w]
        w_v = wl[:, o0 + 2 * kv_w:o0 + 3 * kv_w]
        w_kv = wl[:, o0 + kv_w:o0 + 3 * kv_w]
        w_o = w_out[l].astype(BF16)
        w_g, w_u, w_d = (w.astype(BF16) for w in (w_ffn_gate[l], w_ffn_up[l], w_ffn_down[l]))
        nw = norm_mix_pre[l]

        def finish(x, o_ret, o_diff, gates, tm):
            x = merge(x, o_ret, o_diff, gates, ret_norm[l], diff_norm[l], w_o, norm_mix_post[l], lam_init, tm)
            return ffn(x, norm_ffn_pre[l], w_g, w_u, w_d, norm_ffn_post[l], tm, 2)

        xm = meta_tokens.astype(F32)
        (ret_m,) = norm_matmul(xm, nw, w_ret, (BF16,), n_meta, 1024)
        (kv32_m,) = norm_matmul(xm, nw, w_kv, (F32,), n_meta, 1024)
        zero_state = jnp.zeros((1, RET_HEADS, RET_DK, d // RET_HEADS), F32)
        _, s_meta = retention(log_g, ret_m[None], zero_state, n_meta, n_meta)
        kv16_m = jnp.pad(kv32_m.astype(BF16), ((0, meta_pad - n_meta), (0, 0)))
        k_meta, vt_meta = kv16_m[:, :kv_w], kv16_m[:, kv_w:].T

        (ret_p,) = norm_matmul(xp, nw, w_ret, (BF16,), 1024, 1024)
        (gates_p,) = norm_matmul(xp, nw, w_gate, (BF16,), 1024, 1024)
        (q_p,) = norm_matmul(xp, nw, w_q, (BF16,), 1024, 1024)
        k32_p, k16_p = norm_matmul(xp, nw, w_k, (F32, BF16), 1024, 1024)
        v32_p, vt_p = value_proj(xp, nw, w_v, min(KV_CHUNK, seq), min(KV_CHUNK, seq))
        o_ret, s_fin = retention(log_g, ret_p.reshape(b, seq, -1), s_meta, 128, 128)
        o_diff = prompt_attention(slopes, lam_params, q_p.reshape(b, seq, kv_w), k16_p.reshape(b, seq, kv_w),
                                  vt_p, k_meta, vt_meta, lam_init)
        xp = finish(xp, o_ret.reshape(b * seq, d), o_diff.reshape(b * seq, d), gates_p, 512)
        for o, main, lo in ((outs[0], k32_p, 0), (outs[1], v32_p, kv_w)):
            meta_part = jnp.broadcast_to(kv32_m[None, :, lo:lo + kv_w], (b, n_meta, kv_w))
            both = jnp.concatenate([meta_part, main.reshape(b, seq, kv_w)], axis=1)
            o.append(both.reshape(b, n_meta + seq, DIFF_HEADS, HEAD_W))
        outs[2].append(s_fin)

        (ret_s,) = norm_matmul(xs, nw, w_ret, (BF16,), 512, 1024)
        (gates_s,) = norm_matmul(xs, nw, w_gate, (BF16,), 512, 1024)
        (q_s,) = norm_matmul(xs, nw, w_q, (F32,), 512, 1024)
        (kv32_s,) = norm_matmul(xs, nw, w_kv, (F32,), 512, 1024)
        ret_s = jnp.pad(ret_s.reshape(db, t_new, -1), ((0, 0), (0, pad_rows - t_new), (0, 0)))
        o_ret_s, s_new = retention(log_g, ret_s, state_ret[l], pad_rows, t_new)
        q_s = q_s.reshape(db, t_new, DIFF_HEADS, 1, HEAD_W).transpose(0, 2, 3, 1, 4)
        q_rows = jnp.broadcast_to(q_s, (db, DIFF_HEADS, 2, t_new, HEAD_W)).reshape(db, -1, HEAD_W)
        kv32_s = kv32_s.reshape(db, t_new, 2 * kv_w)
        to_rows = lambda a: jnp.pad(a.reshape(db, t_new * DIFF_HEADS, HEAD_W),
                                    ((0, 0), (0, new_pad - t_new * DIFF_HEADS), (0, 0)))
        n_rows = decode_attention(page_table, lam_params, slope_rows, q_rows, cache_k, cache_v, l,
                                  to_rows(kv32_s[:, :, :kv_w]), to_rows(kv32_s[:, :, kv_w:]), lam_init, t_new)
        o_diff_s = n_rows.reshape(db, DIFF_HEADS, 2, t_new, HEAD_W)[:, :, 0].transpose(0, 2, 1, 3)
        xs = finish(xs, o_ret_s[:, :t_new].reshape(db * t_new, d), o_diff_s.reshape(db * t_new, d), gates_s, 512)
        outs[3].append(kv32_s[:, :, :kv_w].reshape(db, t_new, DIFF_HEADS, HEAD_W))
        outs[4].append(kv32_s[:, :, kv_w:].reshape(db, t_new, DIFF_HEADS, HEAD_W))
        outs[5].append(s_new)

    kp, vp, sp, ks, vs, ss = (jnp.stack(o) for o in outs)
    return (xp.reshape(b, seq, d), xs.reshape(db, t_new, d), kp, vp, sp, ks, vs, ss)
```

```python
import functools
import math

import jax
import jax.numpy as jnp
from jax import lax
from jax.experimental import pallas as pl
from jax.experimental.pallas import tpu as pltpu

F32 = jnp.float32
BF16 = jnp.bfloat16

RMS_EPS = 1e-6
N_META = 16
RET_HEADS = 4
RET_DK = 128
DIFF_HEADS = 8
DIFF_DH = 64
HEAD_W = 2 * DIFF_DH
ATT_BLOCK = 1024
KV_CHUNK = 1024
SUM_ROWS = 16
DECODE_PAGES_PER_STEP = 8
VMEM_LIMIT = 56 * 1024 * 1024
NT_DIMS = (((1,), (1,)), ((), ()))
TN_DIMS = (((0,), (0,)), ((), ()))


def _params(semantics):
    return pltpu.CompilerParams(dimension_semantics=semantics, vmem_limit_bytes=VMEM_LIMIT)


def _resident(shape):
    return pl.BlockSpec(shape, lambda *_: (0,) * len(shape), pipeline_mode=pl.Buffered(1))


def _div_pow2(x, n):
    assert n & (n - 1) == 0
    return lax.shift_right_logical(x, n.bit_length() - 1)


def _mod_pow2(x, n):
    assert n & (n - 1) == 0
    return x & (n - 1)


def _rms(x, w):
    ms = jnp.mean(x * x, axis=-1, keepdims=True)
    return x * lax.rsqrt(ms + RMS_EPS) * w


def _norm_matmul_kernel(x_ref, nw_ref, w_ref, *refs):
    out_refs, h_ref = refs[:-1], refs[-1]

    @pl.when(pl.program_id(1) == 0)
    def _():
        h_ref[...] = _rms(x_ref[...], nw_ref[...]).astype(BF16)

    acc = jnp.dot(h_ref[...], w_ref[...], preferred_element_type=F32)
    for o in out_refs:
        o[...] = acc.astype(o.dtype)


def norm_matmul(x, nw, w, out_dtypes, tm, tn):
    t, k = x.shape
    n = w.shape[1]
    tm = min(tm, t)
    tn = min(tn, n)
    assert t % tm == 0 and n % tn == 0
    return pl.pallas_call(
        _norm_matmul_kernel,
        grid=(t // tm, n // tn),
        in_specs=[
            pl.BlockSpec((tm, k), lambda i, j: (i, 0)),
            pl.BlockSpec((1, k), lambda i, j: (0, 0)),
            pl.BlockSpec((k, tn), lambda i, j: (0, j)),
        ],
        out_specs=[pl.BlockSpec((tm, tn), lambda i, j: (i, j)) for _ in out_dtypes],
        out_shape=[jax.ShapeDtypeStruct((t, n), d) for d in out_dtypes],
        scratch_shapes=[pltpu.VMEM((tm, k), BF16)],
        compiler_params=_params(("parallel", "arbitrary")),
        name="norm_matmul",
    )(x, nw.reshape(1, k), w)


def _value_proj_kernel(x_ref, nw_ref, w_ref, o32_ref, vt_ref):
    h = _rms(x_ref[...], nw_ref[...]).astype(BF16)
    acc = jnp.dot(h, w_ref[...], preferred_element_type=F32)
    o32_ref[...] = acc
    heads, chunks, dv, ck = vt_ref.shape
    for hh in range(heads):
        for cc in range(chunks):
            blk = acc[cc * ck:(cc + 1) * ck, hh * dv:(hh + 1) * dv]
            vt_ref[hh, cc] = jnp.transpose(blk).astype(BF16)


def value_proj(x, nw, w, tm, ck):
    t, k = x.shape
    n = w.shape[1]
    assert t % tm == 0 and tm % ck == 0 and n == DIFF_HEADS * HEAD_W
    return pl.pallas_call(
        _value_proj_kernel,
        grid=(t // tm,),
        in_specs=[pl.BlockSpec((tm, k), lambda i: (i, 0)), _resident((1, k)), _resident((k, n))],
        out_specs=[
            pl.BlockSpec((tm, n), lambda i: (i, 0)),
            pl.BlockSpec((DIFF_HEADS, tm // ck, HEAD_W, ck), lambda i: (0, i, 0, 0)),
        ],
        out_shape=[
            jax.ShapeDtypeStruct((t, n), F32),
            jax.ShapeDtypeStruct((DIFF_HEADS, t // ck, HEAD_W, ck), BF16),
        ],
        compiler_params=_params(("parallel",)),
        name="value_proj",
    )(x, nw.reshape(1, k), w)


def _retention_kernel(logg_ref, qkv_ref, s0_ref, o_ref, sfin_ref, state_ref, *, n_valid):
    c = pl.program_id(1)
    chunk = qkv_ref.shape[1]
    heads, dk, dv = state_ref.shape

    @pl.when(c == 0)
    def _():
        state_ref[...] = s0_ref[0]

    row = lax.broadcasted_iota(jnp.int32, (chunk, chunk), 0)
    col = lax.broadcasted_iota(jnp.int32, (chunk, chunk), 1)
    rel = (row - col).astype(F32)
    idx = lax.broadcasted_iota(jnp.int32, (chunk, 1), 0).astype(F32)
    for h in range(heads):
        lg = logg_ref[h]
        q = qkv_ref[0, :, h * dk:(h + 1) * dk].astype(F32)
        k = qkv_ref[0, :, (heads + h) * dk:(heads + h + 1) * dk].astype(F32) * (dk ** -0.5)
        v = qkv_ref[0, :, 2 * heads * dk + h * dv:2 * heads * dk + (h + 1) * dv]
        decay_in = jnp.where(rel >= 0, jnp.exp(jnp.maximum(rel, 0.0) * lg), 0.0)
        q_dec = jnp.exp((idx + 1.0) * lg)
        k_dec = jnp.exp((n_valid - 1.0 - idx) * lg)
        s_dec = jnp.exp(jnp.full((1, 1), float(n_valid), F32) * lg)

        state = state_ref[h]
        scores = lax.dot_general(q.astype(BF16), k.astype(BF16), NT_DIMS, preferred_element_type=F32) * decay_in
        o = jnp.dot(scores.astype(BF16), v, preferred_element_type=F32)
        o = o + jnp.dot((q * q_dec).astype(BF16), state.astype(BF16), preferred_element_type=F32)
        o_ref[0, :, h * dv:(h + 1) * dv] = o.astype(o_ref.dtype)
        state_ref[h] = s_dec * state + lax.dot_general((k * k_dec).astype(BF16), v, TN_DIMS,
                                                       preferred_element_type=F32)

    @pl.when(c == pl.num_programs(1) - 1)
    def _():
        sfin_ref[0] = state_ref[...]


def retention(log_g, qkv, s0, chunk, n_valid):
    b, t, w = qkv.shape
    dv = (w - 2 * RET_HEADS * RET_DK) // RET_HEADS
    per_batch_state = s0.shape[0] == b and b > 1
    s0_map = (lambda bi, c, lg: (bi, 0, 0, 0)) if per_batch_state else (lambda bi, c, lg: (0, 0, 0, 0))
    state_shape = (RET_HEADS, RET_DK, dv)
    grid_spec = pltpu.PrefetchScalarGridSpec(
        num_scalar_prefetch=1,
        grid=(b, t // chunk),
        in_specs=[
            pl.BlockSpec((1, chunk, w), lambda bi, c, lg: (bi, c, 0)),
            pl.BlockSpec((1,) + state_shape, s0_map),
        ],
        out_specs=[
            pl.BlockSpec((1, chunk, RET_HEADS * dv), lambda bi, c, lg: (bi, c, 0)),
            pl.BlockSpec((1,) + state_shape, lambda bi, c, lg: (bi, 0, 0, 0)),
        ],
        scratch_shapes=[pltpu.VMEM(state_shape, F32)],
    )
    return pl.pallas_call(
        functools.partial(_retention_kernel, n_valid=n_valid),
        grid_spec=grid_spec,
        out_shape=[
            jax.ShapeDtypeStruct((b, t, RET_HEADS * dv), BF16),
            jax.ShapeDtypeStruct((b,) + state_shape, F32),
        ],
        compiler_params=_params(("parallel", "arbitrary")),
        name="retention",
    )(log_g, qkv, s0)


def _lambda_value(lam_ref, lam_init):
    lp = lam_ref[...]
    s1 = jnp.sum(lp[0:1] * lp[1:2], axis=-1, keepdims=True)
    s2 = jnp.sum(lp[2:3] * lp[3:4], axis=-1, keepdims=True)
    return jnp.exp(s1) - jnp.exp(s2) + lam_init


def _prompt_attn_kernel(slope_ref, lam_ref, q_ref, k_ref, vt_ref, km_ref, vtm_ref, o_ref,
                        m_ref, acc_ref, sa_ref, sb_ref, *, lam_init):
    h = pl.program_id(1)
    qi = pl.program_id(2)
    tq = q_ref.shape[1]
    ck = vt_ref.shape[3]
    n_meta_pad = km_ref.shape[0]
    slope = slope_ref[h]

    lane = lax.broadcasted_iota(jnp.int32, (tq, HEAD_W), 1)
    qs = q_ref[0].astype(F32) * (DIFF_DH ** -0.5)
    one_rows = (lax.broadcasted_iota(jnp.int32, (HEAD_W, tq), 0) < 2).astype(F32)
    qa1 = jnp.concatenate([jnp.transpose(jnp.where(lane < DIFF_DH, qs, 0.0)), one_rows], axis=0).astype(BF16)
    qa2 = jnp.concatenate([jnp.transpose(jnp.where(lane >= DIFF_DH, qs, 0.0)), one_rows], axis=0).astype(BF16)
    qa = jnp.concatenate([qa1, qa2], axis=1)

    j_row = lax.broadcasted_iota(jnp.int32, (ck, HEAD_W), 0)
    j_col = lax.broadcasted_iota(jnp.int32, (ck, HEAD_W), 1)
    j_lo = _mod_pow2(j_row, 256).astype(F32)
    j_hi = (j_row - _mod_pow2(j_row, 256)).astype(F32)
    bias_cols = (slope * jnp.where(j_col == 0, j_lo, jnp.where(j_col == 1, j_hi, 0.0))).astype(BF16)
    ones_rows = jnp.ones((SUM_ROWS, ck), BF16)

    m_ref[...] = jnp.full(m_ref.shape, -jnp.inf, F32)
    acc_ref[...] = jnp.zeros(acc_ref.shape, F32)

    def scores(k_rows):
        return jnp.dot(jnp.concatenate([k_rows, bias_cols[:k_rows.shape[0]]], axis=1), qa,
                       preferred_element_type=F32)

    def softmax_pv(s, vt, off, mask):
        shift = -slope * jnp.asarray(off, jnp.int32).astype(F32)
        if mask is not None:
            s = jnp.where(jnp.concatenate([mask, mask], axis=1), s, -jnp.inf)
        m_old = m_ref[...]
        m_new = jnp.maximum(m_old, jnp.max(s, axis=0, keepdims=True) + shift)
        alpha = jnp.exp(m_old - m_new)
        p = jnp.exp(s - (m_new - shift)).astype(BF16)
        v_aug = jnp.concatenate([vt, ones_rows[:, :vt.shape[1]]], axis=0)
        acc_ref[...] = alpha * acc_ref[...] + jnp.dot(v_aug, p, preferred_element_type=F32)
        m_ref[...] = m_new

    def chunk_scores(c, dst_ref):
        dst_ref[...] = scores(k_ref[0, pl.ds(pl.multiple_of(c * ck, ck), ck), :])

    def chunk_update(src_ref, c):
        softmax_pv(src_ref[...], vt_ref[0, c], qi * tq - c * ck, None)

    def causal_chunk_update(src_ref, c):
        v_dyn = qi - c * (ck // tq)
        for v in range(ck // tq):
            @pl.when(v_dyn == v)
            def _():
                kk = (v + 1) * tq
                key_minus_query = (lax.broadcasted_iota(jnp.int32, (kk, tq), 0)
                                   - lax.broadcasted_iota(jnp.int32, (kk, tq), 1))
                softmax_pv(src_ref[:kk, :], vt_ref[0, c, :, :kk], v * tq, key_minus_query <= v * tq)

    key_is_meta = lax.broadcasted_iota(jnp.int32, (n_meta_pad, tq), 0) < N_META
    softmax_pv(scores(km_ref[...]), vtm_ref[...], N_META + qi * tq, key_is_meta)

    n_full = _div_pow2(qi, ck // tq)
    n_pairs = _div_pow2(n_full, 2)
    chunk_scores(0, sa_ref)

    def chunk_pair(j, carry):
        c = 2 * j
        chunk_scores(c + 1, sb_ref)
        chunk_update(sa_ref, c)
        chunk_scores(c + 2, sa_ref)
        chunk_update(sb_ref, c + 1)
        return carry

    lax.fori_loop(0, n_pairs, chunk_pair, 0)

    @pl.when(n_full == 2 * n_pairs)
    def _():
        causal_chunk_update(sa_ref, n_full)

    @pl.when(n_full != 2 * n_pairs)
    def _():
        chunk_scores(n_full, sb_ref)
        chunk_update(sa_ref, n_full - 1)
        causal_chunk_update(sb_ref, n_full)

    lam = _lambda_value(lam_ref, lam_init)
    n = acc_ref[:HEAD_W, :] / acc_ref[HEAD_W:HEAD_W + 1, :]
    o_t = n[:, :tq] - lam * n[:, tq:]
    o_ref[0] = jnp.transpose(o_t).astype(o_ref.dtype)


def prompt_attention(slopes, lam_params, q, k, vt, k_meta, vt_meta, lam_init):
    b, t, _ = q.shape
    tq = ATT_BLOCK
    ck = vt.shape[3]
    n_meta_pad = k_meta.shape[0]
    assert t % ck == 0 and ck % tq == 0
    grid_spec = pltpu.PrefetchScalarGridSpec(
        num_scalar_prefetch=1,
        grid=(b, DIFF_HEADS, t // tq),
        in_specs=[
            pl.BlockSpec(lam_params.shape, lambda bi, h, qi, sl: (0, 0)),
            pl.BlockSpec((1, tq, HEAD_W), lambda bi, h, qi, sl: (bi, qi, h)),
            pl.BlockSpec((1, t, HEAD_W), lambda bi, h, qi, sl: (bi, 0, h)),
            pl.BlockSpec((1, t // ck, HEAD_W, ck), lambda bi, h, qi, sl: (h, bi, 0, 0)),
            pl.BlockSpec((n_meta_pad, HEAD_W), lambda bi, h, qi, sl: (0, h)),
            pl.BlockSpec((HEAD_W, n_meta_pad), lambda bi, h, qi, sl: (h, 0)),
        ],
        out_specs=pl.BlockSpec((1, tq, HEAD_W), lambda bi, h, qi, sl: (bi, qi, h)),
        scratch_shapes=[
            pltpu.VMEM((1, 2 * tq), F32), pltpu.VMEM((HEAD_W + SUM_ROWS, 2 * tq), F32),
            pltpu.VMEM((ck, 2 * tq), F32), pltpu.VMEM((ck, 2 * tq), F32),
        ],
    )
    return pl.pallas_call(
        functools.partial(_prompt_attn_kernel, lam_init=lam_init),
        grid_spec=grid_spec,
        out_shape=jax.ShapeDtypeStruct((b, t, DIFF_HEADS * HEAD_W), BF16),
        compiler_params=_params(("parallel", "parallel", "arbitrary")),
        name="prompt_attn",
    )(slopes, lam_params, q, k, vt, k_meta, vt_meta)


def _decode_attn_kernel(pt_ref, lam_ref, srow_ref, q_ref, *refs, lam_init, t_new, past, n_par):
    k_refs, v_refs = refs[:n_par], refs[n_par:2 * n_par]
    kn_ref, vn_ref, o_ref, m_ref, l_ref, a_ref, b0_ref = refs[2 * n_par:]
    g = pl.program_id(1)
    rows = q_ref.shape[1]
    page, heads = k_refs[0].shape[0], k_refs[0].shape[1]
    nk = page * heads
    srow = srow_ref[:, 0:1]

    def row_parts(shape):
        r = lax.broadcasted_iota(jnp.int32, shape, 0)
        return _div_pow2(r, 2 * t_new), _mod_pow2(r, t_new)

    def col_parts(shape):
        c = lax.broadcasted_iota(jnp.int32, shape, 1)
        return _div_pow2(c, heads), _mod_pow2(c, heads)

    @pl.when(g == 0)
    def _():
        m_ref[...] = jnp.full(m_ref.shape, -jnp.inf, F32)
        l_ref[...] = jnp.zeros(l_ref.shape, F32)
        a_ref[...] = jnp.zeros(a_ref.shape, F32)
        r_head, r_tok = row_parts((rows, n_par * nk))
        c_tok, c_head = col_parts((rows, n_par * nk))
        b0_ref[...] = jnp.where(r_head == c_head, srow * (c_tok - r_tok).astype(F32), -jnp.inf)

    row = lax.broadcasted_iota(jnp.int32, (rows, HEAD_W), 0)
    lane = lax.broadcasted_iota(jnp.int32, (rows, HEAD_W), 1)
    first_map = _mod_pow2(_div_pow2(row, t_new), 2) == 0
    qa = jnp.where(first_map == (lane < DIFF_DH), q_ref[0] * (DIFF_DH ** -0.5), 0.0).astype(BF16)

    def update(t, shift, v16):
        m_old = m_ref[...]
        m_new = jnp.maximum(m_old, jnp.max(t, axis=-1, keepdims=True) + shift)
        alpha = jnp.exp(m_old - m_new)
        pr = jnp.exp(t - (m_new - shift))
        l_ref[...] = alpha * l_ref[...] + jnp.sum(pr, axis=-1, keepdims=True)
        a_ref[...] = alpha * a_ref[...] + jnp.dot(pr.astype(BF16), v16, preferred_element_type=F32)
        m_ref[...] = m_new

    k16 = jnp.concatenate([r[...].reshape(nk, HEAD_W).astype(BF16) for r in k_refs], axis=0)
    v16 = jnp.concatenate([r[...].reshape(nk, HEAD_W).astype(BF16) for r in v_refs], axis=0)
    s = lax.dot_general(qa, k16, NT_DIMS, preferred_element_type=F32)
    update(s + b0_ref[...], -srow * (past - g * (n_par * page)).astype(F32), v16)

    @pl.when(g == pl.num_programs(1) - 1)
    def _():
        n_new = kn_ref.shape[1]
        r_head, r_tok = row_parts((rows, n_new))
        c_tok, c_head = col_parts((rows, n_new))
        visible = (r_head == c_head) & (c_tok <= r_tok)
        s = lax.dot_general(qa, kn_ref[0].astype(BF16), NT_DIMS, preferred_element_type=F32)
        t = jnp.where(visible, s + srow * (c_tok - r_tok).astype(F32), -jnp.inf)
        update(t, jnp.zeros((), F32), vn_ref[0].astype(BF16))
        n = a_ref[...] / l_ref[...]
        lam = _lambda_value(lam_ref, lam_init)
        o_ref[0] = n - lam * pltpu.roll(n, rows - t_new, 0)


def decode_attention(page_table, lam_params, slope_rows, q_rows, cache_k, cache_v, layer, k_new, v_new,
                     lam_init, t_new):
    b, rows, _ = q_rows.shape
    page, heads = cache_k.shape[2], cache_k.shape[3]
    n_pages = page_table.shape[1]
    n_par = DECODE_PAGES_PER_STEP
    assert n_pages % n_par == 0

    def cache_spec(i):
        return pl.BlockSpec((None, None, page, heads, HEAD_W),
                            lambda bi, g, pt: (layer, pt[bi, g * n_par + i], 0, 0, 0))

    per_seq = lambda bi, g, pt: (bi, 0, 0)
    grid_spec = pltpu.PrefetchScalarGridSpec(
        num_scalar_prefetch=1,
        grid=(b, n_pages // n_par),
        in_specs=[
            pl.BlockSpec(lam_params.shape, lambda bi, g, pt: (0, 0)),
            pl.BlockSpec(slope_rows.shape, lambda bi, g, pt: (0, 0)),
            pl.BlockSpec((1, rows, HEAD_W), per_seq),
            *[cache_spec(i) for i in range(n_par)],
            *[cache_spec(i) for i in range(n_par)],
            pl.BlockSpec((1,) + k_new.shape[1:], per_seq),
            pl.BlockSpec((1,) + v_new.shape[1:], per_seq),
        ],
        out_specs=pl.BlockSpec((1, rows, HEAD_W), per_seq),
        scratch_shapes=[
            pltpu.VMEM((rows, 1), F32), pltpu.VMEM((rows, 1), F32), pltpu.VMEM((rows, HEAD_W), F32),
            pltpu.VMEM((rows, n_par * page * heads), F32),
        ],
    )
    return pl.pallas_call(
        functools.partial(_decode_attn_kernel, lam_init=lam_init, t_new=t_new, past=n_pages * page, n_par=n_par),
        grid_spec=grid_spec,
        out_shape=jax.ShapeDtypeStruct((b, rows, HEAD_W), F32),
        compiler_params=_params(("parallel", "arbitrary")),
        name="decode_attn",
    )(page_table, lam_params, slope_rows, q_rows, *([cache_k] * n_par), *([cache_v] * n_par), k_new, v_new)


def _merge_kernel(x_ref, oret_ref, odiff_ref, g_ref, rnw_ref, dnw_ref, wout_ref, pnw_ref, o_ref, m_ref,
                  *, lam_init):
    d = x_ref.shape[1]
    ret_dv = rnw_ref.shape[1]
    for hh in range(d // ret_dv):
        sl = slice(hh * ret_dv, (hh + 1) * ret_dv)
        rg = g_ref[:, sl].astype(F32)
        a = _rms(oret_ref[:, sl].astype(F32), rnw_ref[...]) * (rg * jax.nn.sigmoid(rg))
        ga = g_ref[:, d + hh * ret_dv:d + (hh + 1) * ret_dv].astype(F32)
        m_ref[:, sl] = jax.nn.sigmoid(ga) * a
    diff_dv = dnw_ref.shape[1]
    for hh in range(d // diff_dv):
        sl = slice(hh * diff_dv, (hh + 1) * diff_dv)
        dd = _rms(odiff_ref[:, sl].astype(F32), dnw_ref[...]) * (1.0 - lam_init)
        gb = g_ref[:, 2 * d + hh * diff_dv:2 * d + (hh + 1) * diff_dv].astype(F32)
        m_ref[:, sl] = m_ref[:, sl] + jax.nn.sigmoid(gb) * dd
    mix = jnp.dot(m_ref[...].astype(BF16), wout_ref[...], preferred_element_type=F32)
    o_ref[...] = x_ref[...] + _rms(mix, pnw_ref[...])


def merge(x, o_ret, o_diff, gates, ret_nw, diff_nw, w_out, post_nw, lam_init, tm):
    t, d = x.shape
    tm = min(tm, t)
    row = lambda i: (i, 0)
    return pl.pallas_call(
        functools.partial(_merge_kernel, lam_init=lam_init),
        grid=(t // tm,),
        in_specs=[
            pl.BlockSpec((tm, d), row), pl.BlockSpec((tm, d), row), pl.BlockSpec((tm, d), row),
            pl.BlockSpec((tm, 3 * d), row),
            _resident((1, ret_nw.shape[0])), _resident((1, diff_nw.shape[0])),
            _resident((d, d)), _resident((1, d)),
        ],
        out_specs=pl.BlockSpec((tm, d), row),
        out_shape=jax.ShapeDtypeStruct((t, d), F32),
        scratch_shapes=[pltpu.VMEM((tm, d), F32)],
        compiler_params=_params(("parallel",)),
        name="merge",
    )(x, o_ret, o_diff, gates, ret_nw.reshape(1, -1), diff_nw.reshape(1, -1), w_out, post_nw.reshape(1, -1))


def _ffn_kernel(x_ref, pre_ref, wg_ref, wu_ref, wd_ref, post_ref, o_ref, *, n_split):
    x = x_ref[...]
    h = _rms(x, pre_ref[...]).astype(BF16)
    dff = wg_ref.shape[1]
    step = dff // n_split
    y = jnp.zeros(x.shape, F32)
    for s in range(n_split):
        sl = slice(s * step, (s + 1) * step)
        g = jnp.dot(h, wg_ref[:, sl], preferred_element_type=F32)
        u = jnp.dot(h, wu_ref[:, sl], preferred_element_type=F32)
        act = (g * jax.nn.sigmoid(g) * u).astype(BF16)
        y = y + jnp.dot(act, wd_ref[sl, :], preferred_element_type=F32)
    o_ref[...] = x + _rms(y, post_ref[...])


def ffn(x, pre_nw, w_gate, w_up, w_down, post_nw, tm, n_split):
    t, d = x.shape
    dff = w_gate.shape[1]
    tm = min(tm, t)
    row = lambda i: (i, 0)
    return pl.pallas_call(
        functools.partial(_ffn_kernel, n_split=n_split),
        grid=(t // tm,),
        in_specs=[
            pl.BlockSpec((tm, d), row), _resident((1, d)),
            _resident((d, dff)), _resident((d, dff)), _resident((dff, d)),
            _resident((1, d)),
        ],
        out_specs=pl.BlockSpec((tm, d), row),
        out_shape=jax.ShapeDtypeStruct((t, d), F32),
        compiler_params=_params(("parallel",)),
        name="ffn",
    )(x, pre_nw.reshape(1, -1), w_gate, w_up, w_down, post_nw.reshape(1, -1))


def kernel(x_prompt, x_sample, cache_k, cache_v, state_ret, page_table, meta_tokens, norm_mix_pre,
           norm_mix_post, w_in, ret_norm, lambda_q1, lambda_k1, lambda_q2, lambda_k2, diff_norm, w_out,
           norm_ffn_pre, norm_ffn_post, w_ffn_gate, w_ffn_up, w_ffn_down):
    b, seq, d = x_prompt.shape
    db, t_new, _ = x_sample.shape
    depth = cache_k.shape[0]
    ret_qk = RET_HEADS * RET_DK
    kv_w = DIFF_HEADS * HEAD_W
    n_meta = meta_tokens.shape[0]
    assert n_meta == N_META
    log_g = jnp.log1p(-jnp.exp2(-5.0 - jnp.arange(RET_HEADS, dtype=F32)))
    slopes = jnp.exp2(-8.0 * (jnp.arange(DIFF_HEADS, dtype=F32) + 1.0) / DIFF_HEADS)
    pad_rows = 16
    meta_pad = 128
    new_pad = 128
    slope_rows = jnp.broadcast_to(jnp.repeat(slopes, 2 * t_new)[:, None], (DIFF_HEADS * 2 * t_new, 128))

    xp = x_prompt.reshape(b * seq, d)
    xs = x_sample.reshape(db * t_new, d)
    outs = [[] for _ in range(6)]
    for l in range(depth):
        lam_init = 0.8 - 0.6 * math.exp(-0.3 * l)
        lam_params = jnp.stack([lambda_q1[l], lambda_k1[l], lambda_q2[l], lambda_k2[l]]).astype(F32)
        wl = w_in[l].astype(BF16)
        o0 = 2 * ret_qk + 2 * d
        w_ret = wl[:, :2 * ret_qk + d]
        w_gate = jnp.concatenate([wl[:, 2 * ret_qk + d:o0], wl[:, o0 + 3 * kv_w:]], axis=1)
        w_q = wl[:, o0:o0 + kv_w]
        w_k = wl[:, o0 + kv_w:o0 + 2 * kv_w]
        w_v = wl[:, o0 + 2 * kv_w:o0 + 3 * kv_w]
        w_kv = wl[:, o0 + kv_w:o0 + 3 * kv_w]
        w_o = w_out[l].astype(BF16)
        w_g, w_u, w_d = (w.astype(BF16) for w in (w_ffn_gate[l], w_ffn_up[l], w_ffn_down[l]))
        nw = norm_mix_pre[l]

        def finish(x, o_ret, o_diff, gates, tm):
            x = merge(x, o_ret, o_diff, gates, ret_norm[l], diff_norm[l], w_o, norm_mix_post[l], lam_init, tm)
            return ffn(x, norm_ffn_pre[l], w_g, w_u, w_d, norm_ffn_post[l], tm, 2)

        xm = meta_tokens.astype(F32)
        (ret_m,) = norm_matmul(xm, nw, w_ret, (BF16,), n_meta, 1024)
        (kv32_m,) = norm_matmul(xm, nw, w_kv, (F32,), n_meta, 1024)
        zero_state = jnp.zeros((1, RET_HEADS, RET_DK, d // RET_HEADS), F32)
        _, s_meta = retention(log_g, ret_m[None], zero_state, n_meta, n_meta)
        kv16_m = jnp.pad(kv32_m.astype(BF16), ((0, meta_pad - n_meta), (0, 0)))
        k_meta, vt_meta = kv16_m[:, :kv_w], kv16_m[:, kv_w:].T

        (ret_p,) = norm_matmul(xp, nw, w_ret, (BF16,), 1024, 1024)
        (gates_p,) = norm_matmul(xp, nw, w_gate, (BF16,), 1024, 1024)
        (q_p,) = norm_matmul(xp, nw, w_q, (BF16,), 1024, 1024)
        k32_p, k16_p = norm_matmul(xp, nw, w_k, (F32, BF16), 1024, 1024)
        v32_p, vt_p = value_proj(xp, nw, w_v, min(KV_CHUNK, seq), min(KV_CHUNK, seq))
        o_ret, s_fin = retention(log_g, ret_p.reshape(b, seq, -1), s_meta, 128, 128)
        o_diff = prompt_attention(slopes, lam_params, q_p.reshape(b, seq, kv_w), k16_p.reshape(b, seq, kv_w),
                                  vt_p, k_meta, vt_meta, lam_init)
        xp = finish(xp, o_ret.reshape(b * seq, d), o_diff.reshape(b * seq, d), gates_p, 512)
        for o, main, lo in ((outs[0], k32_p, 0), (outs[1], v32_p, kv_w)):
            meta_part = jnp.broadcast_to(kv32_m[None, :, lo:lo + kv_w], (b, n_meta, kv_w))
            both = jnp.concatenate([meta_part, main.reshape(b, seq, kv_w)], axis=1)
            o.append(both.reshape(b, n_meta + seq, DIFF_HEADS, HEAD_W))
        outs[2].append(s_fin)

        (ret_s,) = norm_matmul(xs, nw, w_ret, (BF16,), 512, 1024)
        (gates_s,) = norm_matmul(xs, nw, w_gate, (BF16,), 512, 1024)
        (q_s,) = norm_matmul(xs, nw, w_q, (F32,), 512, 1024)
        (kv32_s,) = norm_matmul(xs, nw, w_kv, (F32,), 512, 1024)
        ret_s = jnp.pad(ret_s.reshape(db, t_new, -1), ((0, 0), (0, pad_rows - t_new), (0, 0)))
        o_ret_s, s_new = retention(log_g, ret_s, state_ret[l], pad_rows, t_new)
        q_s = q_s.reshape(db, t_new, DIFF_HEADS, 1, HEAD_W).transpose(0, 2, 3, 1, 4)
        q_rows = jnp.broadcast_to(q_s, (db, DIFF_HEADS, 2, t_new, HEAD_W)).reshape(db, -1, HEAD_W)
        kv32_s = kv32_s.reshape(db, t_new, 2 * kv_w)
        to_rows = lambda a: jnp.pad(a.reshape(db, t_new * DIFF_HEADS, HEAD_W),
                                    ((0, 0), (0, new_pad - t_new * DIFF_HEADS), (0, 0)))
        n_rows = decode_attention(page_table, lam_params, slope_rows, q_rows, cache_k, cache_v, l,
                                  to_rows(kv32_s[:, :, :kv_w]), to_rows(kv32_s[:, :, kv_w:]), lam_init, t_new)
        o_diff_s = n_rows.reshape(db, DIFF_HEADS, 2, t_new, HEAD_W)[:, :, 0].transpose(0, 2, 1, 3)
        xs = finish(xs, o_ret_s[:, :t_new].reshape(db * t_new, d), o_diff_s.reshape(db * t_new, d), gates_s, 512)
        outs[3].append(kv32_s[:, :, :kv_w].reshape(db, t_new, DIFF_HEADS, HEAD_W))
        outs[4].append(kv32_s[:, :, kv_w:].reshape(db, t_new, DIFF_HEADS, HEAD_W))
        outs[5].append(s_new)

    kp, vp, sp, ks, vs, ss = (jnp.stack(o) for o in outs)
    return (xp.reshape(b, seq, d), xs.reshape(db, t_new, d), kp, vp, sp, ks, vs, ss)
```

```python
import functools
import math

import jax
import jax.numpy as jnp
from jax import lax
from jax.experimental import pallas as pl
from jax.experimental.pallas import tpu as pltpu

F32 = jnp.float32
BF16 = jnp.bfloat16

RMS_EPS = 1e-6
N_META = 16
RET_HEADS = 4
RET_DK = 128
DIFF_HEADS = 8
DIFF_DH = 64
HEAD_W = 2 * DIFF_DH
ATT_BLOCK = 1024
KV_CHUNK = 1024
SUM_ROWS = 16
DECODE_PAGES_PER_STEP = 8
VMEM_LIMIT = 56 * 1024 * 1024
NT_DIMS = (((1,), (1,)), ((), ()))
TN_DIMS = (((0,), (0,)), ((), ()))


def _params(semantics):
    return pltpu.CompilerParams(dimension_semantics=semantics, vmem_limit_bytes=VMEM_LIMIT)


def _resident(shape):
    return pl.BlockSpec(shape, lambda *_: (0,) * len(shape), pipeline_mode=pl.Buffered(1))


def _div_pow2(x, n):
    assert n & (n - 1) == 0
    return lax.shift_right_logical(x, n.bit_length() - 1)


def _mod_pow2(x, n):
    assert n & (n - 1) == 0
    return x & (n - 1)


def _rms(x, w):
    ms = jnp.mean(x * x, axis=-1, keepdims=True)
    return x * lax.rsqrt(ms + RMS_EPS) * w


def _norm_matmul_kernel(x_ref, nw_ref, w_ref, *refs, column_steps):
    if column_steps == 1:
        out_refs = refs
        h = _rms(x_ref[...], nw_ref[...]).astype(BF16)
    else:
        out_refs, h_ref = refs[:-1], refs[-1]

        @pl.when(pl.program_id(1) == 0)
        def _():
            h_ref[...] = _rms(x_ref[...], nw_ref[...]).astype(BF16)

        h = h_ref[...]
    acc = jnp.dot(h, w_ref[...], preferred_element_type=F32)
    for o in out_refs:
        o[...] = acc.astype(o.dtype)


def norm_matmul(x, nw, w, out_dtypes, tm, tn):
    t, k = x.shape
    n = w.shape[1]
    tm = min(tm, t)
    tn = min(tn, n)
    assert t % tm == 0 and n % tn == 0
    column_steps = n // tn
    return pl.pallas_call(
        functools.partial(_norm_matmul_kernel, column_steps=column_steps),
        grid=(t // tm, column_steps),
        in_specs=[
            pl.BlockSpec((tm, k), lambda i, j: (i, 0)),
            pl.BlockSpec((1, k), lambda i, j: (0, 0)),
            pl.BlockSpec((k, tn), lambda i, j: (0, j)),
        ],
        out_specs=[pl.BlockSpec((tm, tn), lambda i, j: (i, j)) for _ in out_dtypes],
        out_shape=[jax.ShapeDtypeStruct((t, n), d) for d in out_dtypes],
        scratch_shapes=[pltpu.VMEM((tm, k), BF16)] if column_steps > 1 else [],
        compiler_params=_params(("parallel", "arbitrary")),
        name="norm_matmul",
    )(x, nw.reshape(1, k), w)


def _value_proj_kernel(x_ref, nw_ref, w_ref, o32_ref, vt_ref):
    h = _rms(x_ref[...], nw_ref[...]).astype(BF16)
    acc = jnp.dot(h, w_ref[...], preferred_element_type=F32)
    o32_ref[...] = acc
    heads, chunks, dv, ck = vt_ref.shape
    for hh in range(heads):
        for cc in range(chunks):
            blk = acc[cc * ck:(cc + 1) * ck, hh * dv:(hh + 1) * dv]
            vt_ref[hh, cc] = jnp.transpose(blk).astype(BF16)


def value_proj(x, nw, w, tm, ck):
    t, k = x.shape
    n = w.shape[1]
    assert t % tm == 0 and tm % ck == 0 and n == DIFF_HEADS * HEAD_W
    return pl.pallas_call(
        _value_proj_kernel,
        grid=(t // tm,),
        in_specs=[pl.BlockSpec((tm, k), lambda i: (i, 0)), _resident((1, k)), _resident((k, n))],
        out_specs=[
            pl.BlockSpec((tm, n), lambda i: (i, 0)),
            pl.BlockSpec((DIFF_HEADS, tm // ck, HEAD_W, ck), lambda i: (0, i, 0, 0)),
        ],
        out_shape=[
            jax.ShapeDtypeStruct((t, n), F32),
            jax.ShapeDtypeStruct((DIFF_HEADS, t // ck, HEAD_W, ck), BF16),
        ],
        compiler_params=_params(("parallel",)),
        name="value_proj",
    )(x, nw.reshape(1, k), w)


def _retention_kernel(logg_ref, qkv_ref, s0_ref, o_ref, sfin_ref, state_ref, *, n_valid):
    c = pl.program_id(1)
    chunk = qkv_ref.shape[1]
    heads, dk, dv = state_ref.shape

    @pl.when(c == 0)
    def _():
        state_ref[...] = s0_ref[0]

    row = lax.broadcasted_iota(jnp.int32, (chunk, chunk), 0)
    col = lax.broadcasted_iota(jnp.int32, (chunk, chunk), 1)
    rel = (row - col).astype(F32)
    idx = lax.broadcasted_iota(jnp.int32, (chunk, 1), 0).astype(F32)
    for h in range(heads):
        lg = logg_ref[h]
        q = qkv_ref[0, :, h * dk:(h + 1) * dk].astype(F32)
        k = qkv_ref[0, :, (heads + h) * dk:(heads + h + 1) * dk].astype(F32) * (dk ** -0.5)
        v = qkv_ref[0, :, 2 * heads * dk + h * dv:2 * heads * dk + (h + 1) * dv]
        decay_in = jnp.where(rel >= 0, jnp.exp(jnp.maximum(rel, 0.0) * lg), 0.0)
        q_dec = jnp.exp((idx + 1.0) * lg)
        k_dec = jnp.exp((n_valid - 1.0 - idx) * lg)
        s_dec = jnp.exp(jnp.full((1, 1), float(n_valid), F32) * lg)

        state = state_ref[h]
        scores = lax.dot_general(q.astype(BF16), k.astype(BF16), NT_DIMS, preferred_element_type=F32) * decay_in
        o = jnp.dot(scores.astype(BF16), v, preferred_element_type=F32)
        o = o + jnp.dot((q * q_dec).astype(BF16), state.astype(BF16), preferred_element_type=F32)
        o_ref[0, :, h * dv:(h + 1) * dv] = o.astype(o_ref.dtype)
        state_ref[h] = s_dec * state + lax.dot_general((k * k_dec).astype(BF16), v, TN_DIMS,
                                                       preferred_element_type=F32)

    @pl.when(c == pl.num_programs(1) - 1)
    def _():
        sfin_ref[0] = state_ref[...]


def retention(log_g, qkv, s0, chunk, n_valid):
    b, t, w = qkv.shape
    dv = (w - 2 * RET_HEADS * RET_DK) // RET_HEADS
    per_batch_state = s0.shape[0] == b and b > 1
    s0_map = (lambda bi, c, lg: (bi, 0, 0, 0)) if per_batch_state else (lambda bi, c, lg: (0, 0, 0, 0))
    state_shape = (RET_HEADS, RET_DK, dv)
    grid_spec = pltpu.PrefetchScalarGridSpec(
        num_scalar_prefetch=1,
        grid=(b, t // chunk),
        in_specs=[
            pl.BlockSpec((1, chunk, w), lambda bi, c, lg: (bi, c, 0)),
            pl.BlockSpec((1,) + state_shape, s0_map),
        ],
        out_specs=[
            pl.BlockSpec((1, chunk, RET_HEADS * dv), lambda bi, c, lg: (bi, c, 0)),
            pl.BlockSpec((1,) + state_shape, lambda bi, c, lg: (bi, 0, 0, 0)),
        ],
        scratch_shapes=[pltpu.VMEM(state_shape, F32)],
    )
    return pl.pallas_call(
        functools.partial(_retention_kernel, n_valid=n_valid),
        grid_spec=grid_spec,
        out_shape=[
            jax.ShapeDtypeStruct((b, t, RET_HEADS * dv), BF16),
            jax.ShapeDtypeStruct((b,) + state_shape, F32),
        ],
        compiler_params=_params(("parallel", "arbitrary")),
        name="retention",
    )(log_g, qkv, s0)


def _lambda_value(lam_ref, lam_init):
    lp = lam_ref[...]
    s1 = jnp.sum(lp[0:1] * lp[1:2], axis=-1, keepdims=True)
    s2 = jnp.sum(lp[2:3] * lp[3:4], axis=-1, keepdims=True)
    return jnp.exp(s1) - jnp.exp(s2) + lam_init


def _prompt_attn_kernel(slope_ref, lam_ref, q_ref, k_ref, vt_ref, km_ref, vtm_ref, o_ref,
                        m_ref, acc_ref, sa_ref, sb_ref, *, lam_init):
    h = pl.program_id(1)
    qi = pl.program_id(2)
    tq = q_ref.shape[1]
    ck = vt_ref.shape[3]
    n_meta_pad = km_ref.shape[0]
    slope = slope_ref[h]

    lane = lax.broadcasted_iota(jnp.int32, (tq, HEAD_W), 1)
    qs = q_ref[0].astype(F32) * (DIFF_DH ** -0.5)
    one_rows = (lax.broadcasted_iota(jnp.int32, (HEAD_W, tq), 0) < 2).astype(F32)
    qa1 = jnp.concatenate([jnp.transpose(jnp.where(lane < DIFF_DH, qs, 0.0)), one_rows], axis=0).astype(BF16)
    qa2 = jnp.concatenate([jnp.transpose(jnp.where(lane >= DIFF_DH, qs, 0.0)), one_rows], axis=0).astype(BF16)
    qa = jnp.concatenate([qa1, qa2], axis=1)

    j_row = lax.broadcasted_iota(jnp.int32, (ck, HEAD_W), 0)
    j_col = lax.broadcasted_iota(jnp.int32, (ck, HEAD_W), 1)
    j_lo = _mod_pow2(j_row, 256).astype(F32)
    j_hi = (j_row - _mod_pow2(j_row, 256)).astype(F32)
    bias_cols = (slope * jnp.where(j_col == 0, j_lo, jnp.where(j_col == 1, j_hi, 0.0))).astype(BF16)
    ones_rows = jnp.ones((SUM_ROWS, ck), BF16)

    m_ref[...] = jnp.full(m_ref.shape, -jnp.inf, F32)
    acc_ref[...] = jnp.zeros(acc_ref.shape, F32)

    def scores(k_rows):
        return jnp.dot(jnp.concatenate([k_rows, bias_cols[:k_rows.shape[0]]], axis=1), qa,
                       preferred_element_type=F32)

    def softmax_pv(s, vt, off, mask):
        shift = -slope * jnp.asarray(off, jnp.int32).astype(F32)
        if mask is not None:
            s = jnp.where(jnp.concatenate([mask, mask], axis=1), s, -jnp.inf)
        m_old = m_ref[...]
        m_new = jnp.maximum(m_old, jnp.max(s, axis=0, keepdims=True) + shift)
        alpha = jnp.exp(m_old - m_new)
        p = jnp.exp(s - (m_new - shift)).astype(BF16)
        v_aug = jnp.concatenate([vt, ones_rows[:, :vt.shape[1]]], axis=0)
        acc_ref[...] = alpha * acc_ref[...] + jnp.dot(v_aug, p, preferred_element_type=F32)
        m_ref[...] = m_new

    def chunk_scores(c, dst_ref):
        dst_ref[...] = scores(k_ref[0, pl.ds(pl.multiple_of(c * ck, ck), ck), :])

    def chunk_update(src_ref, c):
        softmax_pv(src_ref[...], vt_ref[0, c], qi * tq - c * ck, None)

    def causal_chunk_update(src_ref, c):
        v_dyn = qi - c * (ck // tq)
        for v in range(ck // tq):
            @pl.when(v_dyn == v)
            def _():
                kk = (v + 1) * tq
                key_minus_query = (lax.broadcasted_iota(jnp.int32, (kk, tq), 0)
                                   - lax.broadcasted_iota(jnp.int32, (kk, tq), 1))
                softmax_pv(src_ref[:kk, :], vt_ref[0, c, :, :kk], v * tq, key_minus_query <= v * tq)

    key_is_meta = lax.broadcasted_iota(jnp.int32, (n_meta_pad, tq), 0) < N_META
    softmax_pv(scores(km_ref[...]), vtm_ref[...], N_META + qi * tq, key_is_meta)

    n_full = _div_pow2(qi, ck // tq)
    n_pairs = _div_pow2(n_full, 2)
    chunk_scores(0, sa_ref)

    def chunk_pair(j, carry):
        c = 2 * j
        chunk_scores(c + 1, sb_ref)
        chunk_update(sa_ref, c)
        chunk_scores(c + 2, sa_ref)
        chunk_update(sb_ref, c + 1)
        return carry

    lax.fori_loop(0, n_pairs, chunk_pair, 0)

    @pl.when(n_full == 2 * n_pairs)
    def _():
        causal_chunk_update(sa_ref, n_full)

    @pl.when(n_full != 2 * n_pairs)
    def _():
        chunk_scores(n_full, sb_ref)
        chunk_update(sa_ref, n_full - 1)
        causal_chunk_update(sb_ref, n_full)

    lam = _lambda_value(lam_ref, lam_init)
    n = acc_ref[:HEAD_W, :] / acc_ref[HEAD_W:HEAD_W + 1, :]
    o_t = n[:, :tq] - lam * n[:, tq:]
    o_ref[0] = jnp.transpose(o_t).astype(o_ref.dtype)


def prompt_attention(slopes, lam_params, q, k, vt, k_meta, vt_meta, lam_init):
    b, t, _ = q.shape
    tq = ATT_BLOCK
    ck = vt.shape[3]
    n_meta_pad = k_meta.shape[0]
    assert t % ck == 0 and ck % tq == 0
    grid_spec = pltpu.PrefetchScalarGridSpec(
        num_scalar_prefetch=1,
        grid=(b, DIFF_HEADS, t // tq),
        in_specs=[
            pl.BlockSpec(lam_params.shape, lambda bi, h, qi, sl: (0, 0)),
            pl.BlockSpec((1, tq, HEAD_W), lambda bi, h, qi, sl: (bi, qi, h)),
            pl.BlockSpec((1, t, HEAD_W), lambda bi, h, qi, sl: (bi, 0, h)),
            pl.BlockSpec((1, t // ck, HEAD_W, ck), lambda bi, h, qi, sl: (h, bi, 0, 0)),
            pl.BlockSpec((n_meta_pad, HEAD_W), lambda bi, h, qi, sl: (0, h)),
            pl.BlockSpec((HEAD_W, n_meta_pad), lambda bi, h, qi, sl: (h, 0)),
        ],
        out_specs=pl.BlockSpec((1, tq, HEAD_W), lambda bi, h, qi, sl: (bi, qi, h)),
        scratch_shapes=[
            pltpu.VMEM((1, 2 * tq), F32), pltpu.VMEM((HEAD_W + SUM_ROWS, 2 * tq), F32),
            pltpu.VMEM((ck, 2 * tq), F32), pltpu.VMEM((ck, 2 * tq), F32),
        ],
    )
    return pl.pallas_call(
        functools.partial(_prompt_attn_kernel, lam_init=lam_init),
        grid_spec=grid_spec,
        out_shape=jax.ShapeDtypeStruct((b, t, DIFF_HEADS * HEAD_W), BF16),
        compiler_params=_params(("parallel", "parallel", "arbitrary")),
        name="prompt_attn",
    )(slopes, lam_params, q, k, vt, k_meta, vt_meta)


def _decode_attn_kernel(pt_ref, lam_ref, srow_ref, q_ref, *refs, lam_init, t_new, past, n_par):
    k_refs, v_refs = refs[:n_par], refs[n_par:2 * n_par]
    kn_ref, vn_ref, o_ref, m_ref, l_ref, a_ref, b0_ref = refs[2 * n_par:]
    g = pl.program_id(1)
    rows = q_ref.shape[1]
    page, heads = k_refs[0].shape[0], k_refs[0].shape[1]
    nk = page * heads
    srow = srow_ref[:, 0:1]

    def row_parts(shape):
        r = lax.broadcasted_iota(jnp.int32, shape, 0)
        return _div_pow2(r, 2 * t_new), _mod_pow2(r, t_new)

    def col_parts(shape):
        c = lax.broadcasted_iota(jnp.int32, shape, 1)
        return _div_pow2(c, heads), _mod_pow2(c, heads)

    @pl.when(g == 0)
    def _():
        m_ref[...] = jnp.full(m_ref.shape, -jnp.inf, F32)
        l_ref[...] = jnp.zeros(l_ref.shape, F32)
        a_ref[...] = jnp.zeros(a_ref.shape, F32)
        r_head, r_tok = row_parts((rows, n_par * nk))
        c_tok, c_head = col_parts((rows, n_par * nk))
        b0_ref[...] = jnp.where(r_head == c_head, srow * (c_tok - r_tok).astype(F32), -jnp.inf)

    row = lax.broadcasted_iota(jnp.int32, (rows, HEAD_W), 0)
    lane = lax.broadcasted_iota(jnp.int32, (rows, HEAD_W), 1)
    first_map = _mod_pow2(_div_pow2(row, t_new), 2) == 0
    qa = jnp.where(first_map == (lane < DIFF_DH), q_ref[0] * (DIFF_DH ** -0.5), 0.0).astype(BF16)

    def update(t, shift, v16):
        m_old = m_ref[...]
        m_new = jnp.maximum(m_old, jnp.max(t, axis=-1, keepdims=True) + shift)
        alpha = jnp.exp(m_old - m_new)
        pr = jnp.exp(t - (m_new - shift))
        l_ref[...] = alpha * l_ref[...] + jnp.sum(pr, axis=-1, keepdims=True)
        a_ref[...] = alpha * a_ref[...] + jnp.dot(pr.astype(BF16), v16, preferred_element_type=F32)
        m_ref[...] = m_new

    k16 = jnp.concatenate([r[...].reshape(nk, HEAD_W).astype(BF16) for r in k_refs], axis=0)
    v16 = jnp.concatenate([r[...].reshape(nk, HEAD_W).astype(BF16) for r in v_refs], axis=0)
    s = lax.dot_general(qa, k16, NT_DIMS, preferred_element_type=F32)
    update(s + b0_ref[...], -srow * (past - g * (n_par * page)).astype(F32), v16)

    @pl.when(g == pl.num_programs(1) - 1)
    def _():
        n_new = kn_ref.shape[1]
        r_head, r_tok = row_parts((rows, n_new))
        c_tok, c_head = col_parts((rows, n_new))
        visible = (r_head == c_head) & (c_tok <= r_tok)
        s = lax.dot_general(qa, kn_ref[0].astype(BF16), NT_DIMS, preferred_element_type=F32)
        t = jnp.where(visible, s + srow * (c_tok - r_tok).astype(F32), -jnp.inf)
        update(t, jnp.zeros((), F32), vn_ref[0].astype(BF16))
        n = a_ref[...] / l_ref[...]
        lam = _lambda_value(lam_ref, lam_init)
        o_ref[0] = n - lam * pltpu.roll(n, rows - t_new, 0)


def decode_attention(page_table, lam_params, slope_rows, q_rows, cache_k, cache_v, layer, k_new, v_new,
                     lam_init, t_new):
    b, rows, _ = q_rows.shape
    page, heads = cache_k.shape[2], cache_k.shape[3]
    n_pages = page_table.shape[1]
    n_par = DECODE_PAGES_PER_STEP
    assert n_pages % n_par == 0

    def cache_spec(i):
        return pl.BlockSpec((None, None, page, heads, HEAD_W),
                            lambda bi, g, pt: (layer, pt[bi, g * n_par + i], 0, 0, 0))

    per_seq = lambda bi, g, pt: (bi, 0, 0)
    grid_spec = pltpu.PrefetchScalarGridSpec(
        num_scalar_prefetch=1,
        grid=(b, n_pages // n_par),
        in_specs=[
            pl.BlockSpec(lam_params.shape, lambda bi, g, pt: (0, 0)),
            pl.BlockSpec(slope_rows.shape, lambda bi, g, pt: (0, 0)),
            pl.BlockSpec((1, rows, HEAD_W), per_seq),
            *[cache_spec(i) for i in range(n_par)],
            *[cache_spec(i) for i in range(n_par)],
            pl.BlockSpec((1,) + k_new.shape[1:], per_seq),
            pl.BlockSpec((1,) + v_new.shape[1:], per_seq),
        ],
        out_specs=pl.BlockSpec((1, rows, HEAD_W), per_seq),
        scratch_shapes=[
            pltpu.VMEM((rows, 1), F32), pltpu.VMEM((rows, 1), F32), pltpu.VMEM((rows, HEAD_W), F32),
            pltpu.VMEM((rows, n_par * page * heads), F32),
        ],
    )
    return pl.pallas_call(
        functools.partial(_decode_attn_kernel, lam_init=lam_init, t_new=t_new, past=n_pages * page, n_par=n_par),
        grid_spec=grid_spec,
        out_shape=jax.ShapeDtypeStruct((b, rows, HEAD_W), F32),
        compiler_params=_params(("parallel", "arbitrary")),
        name="decode_attn",
    )(page_table, lam_params, slope_rows, q_rows, *([cache_k] * n_par), *([cache_v] * n_par), k_new, v_new)


def _merge_kernel(x_ref, oret_ref, odiff_ref, g_ref, rnw_ref, dnw_ref, wout_ref, pnw_ref, o_ref, m_ref,
                  *, lam_init):
    d = x_ref.shape[1]
    ret_dv = rnw_ref.shape[1]
    for hh in range(d // ret_dv):
        sl = slice(hh * ret_dv, (hh + 1) * ret_dv)
        rg = g_ref[:, sl].astype(F32)
        a = _rms(oret_ref[:, sl].astype(F32), rnw_ref[...]) * (rg * jax.nn.sigmoid(rg))
        ga = g_ref[:, d + hh * ret_dv:d + (hh + 1) * ret_dv].astype(F32)
        m_ref[:, sl] = jax.nn.sigmoid(ga) * a
    diff_dv = dnw_ref.shape[1]
    for hh in range(d // diff_dv):
        sl = slice(hh * diff_dv, (hh + 1) * diff_dv)
        dd = _rms(odiff_ref[:, sl].astype(F32), dnw_ref[...]) * (1.0 - lam_init)
        gb = g_ref[:, 2 * d + hh * diff_dv:2 * d + (hh + 1) * diff_dv].astype(F32)
        m_ref[:, sl] = m_ref[:, sl] + jax.nn.sigmoid(gb) * dd
    mix = jnp.dot(m_ref[...].astype(BF16), wout_ref[...], preferred_element_type=F32)
    o_ref[...] = x_ref[...] + _rms(mix, pnw_ref[...])


def merge(x, o_ret, o_diff, gates, ret_nw, diff_nw, w_out, post_nw, lam_init, tm):
    t, d = x.shape
    tm = min(tm, t)
    row = lambda i: (i, 0)
    return pl.pallas_call(
        functools.partial(_merge_kernel, lam_init=lam_init),
        grid=(t // tm,),
        in_specs=[
            pl.BlockSpec((tm, d), row), pl.BlockSpec((tm, d), row), pl.BlockSpec((tm, d), row),
            pl.BlockSpec((tm, 3 * d), row),
            _resident((1, ret_nw.shape[0])), _resident((1, diff_nw.shape[0])),
            _resident((d, d)), _resident((1, d)),
        ],
        out_specs=pl.BlockSpec((tm, d), row),
        out_shape=jax.ShapeDtypeStruct((t, d), F32),
        scratch_shapes=[pltpu.VMEM((tm, d), F32)],
        compiler_params=_params(("parallel",)),
        name="merge",
    )(x, o_ret, o_diff, gates, ret_nw.reshape(1, -1), diff_nw.reshape(1, -1), w_out, post_nw.reshape(1, -1))


def _ffn_kernel(x_ref, pre_ref, wg_ref, wu_ref, wd_ref, post_ref, o_ref, *, n_split):
    x = x_ref[...]
    h = _rms(x, pre_ref[...]).astype(BF16)
    dff = wg_ref.shape[1]
    step = dff // n_split
    y = jnp.zeros(x.shape, F32)
    for s in range(n_split):
        sl = slice(s * step, (s + 1) * step)
        g = jnp.dot(h, wg_ref[:, sl], preferred_element_type=F32)
        u = jnp.dot(h, wu_ref[:, sl], preferred_element_type=F32)
        act = (g * jax.nn.sigmoid(g) * u).astype(BF16)
        y = y + jnp.dot(act, wd_ref[sl, :], preferred_element_type=F32)
    o_ref[...] = x + _rms(y, post_ref[...])


def ffn(x, pre_nw, w_gate, w_up, w_down, post_nw, tm, n_split):
    t, d = x.shape
    dff = w_gate.shape[1]
    tm = min(tm, t)
    row = lambda i: (i, 0)
    return pl.pallas_call(
        functools.partial(_ffn_kernel, n_split=n_split),
        grid=(t // tm,),
        in_specs=[
            pl.BlockSpec((tm, d), row), _resident((1, d)),
            _resident((d, dff)), _resident((d, dff)), _resident((dff, d)),
            _resident((1, d)),
        ],
        out_specs=pl.BlockSpec((tm, d), row),
        out_shape=jax.ShapeDtypeStruct((t, d), F32),
        compiler_params=_params(("parallel",)),
        name="ffn",
    )(x, pre_nw.reshape(1, -1), w_gate, w_up, w_down, post_nw.reshape(1, -1))


def kernel(x_prompt, x_sample, cache_k, cache_v, state_ret, page_table, meta_tokens, norm_mix_pre,
           norm_mix_post, w_in, ret_norm, lambda_q1, lambda_k1, lambda_q2, lambda_k2, diff_norm, w_out,
           norm_ffn_pre, norm_ffn_post, w_ffn_gate, w_ffn_up, w_ffn_down):
    b, seq, d = x_prompt.shape
    db, t_new, _ = x_sample.shape
    depth = cache_k.shape[0]
    ret_qk = RET_HEADS * RET_DK
    kv_w = DIFF_HEADS * HEAD_W
    n_meta = meta_tokens.shape[0]
    assert n_meta == N_META
    log_g = jnp.log1p(-jnp.exp2(-5.0 - jnp.arange(RET_HEADS, dtype=F32)))
    slopes = jnp.exp2(-8.0 * (jnp.arange(DIFF_HEADS, dtype=F32) + 1.0) / DIFF_HEADS)
    pad_rows = 16
    meta_pad = 128
    new_pad = 128
    slope_rows = jnp.broadcast_to(jnp.repeat(slopes, 2 * t_new)[:, None], (DIFF_HEADS * 2 * t_new, 128))

    xp = x_prompt.reshape(b * seq, d)
    xs = x_sample.reshape(db * t_new, d)
    outs = [[] for _ in range(6)]
    for l in range(depth):
        lam_init = 0.8 - 0.6 * math.exp(-0.3 * l)
        lam_params = jnp.stack([lambda_q1[l], lambda_k1[l], lambda_q2[l], lambda_k2[l]]).astype(F32)
        wl = w_in[l].astype(BF16)
        o0 = 2 * ret_qk + 2 * d
        w_ret = wl[:, :2 * ret_qk + d]
        w_gate = jnp.concatenate([wl[:, 2 * ret_qk + d:o0], wl[:, o0 + 3 * kv_w:]], axis=1)
        w_q = wl[:, o0:o0 + kv_w]
        w_k = wl[:, o0 + kv_w:o0 + 2 * kv_w]
        w_v = wl[:, o0 + 2 * kv_w:o0 + 3 * kv_w]
        w_kv = wl[:, o0 + kv_w:o0 + 3 * kv_w]
        w_o = w_out[l].astype(BF16)
        w_g, w_u, w_d = (w.astype(BF16) for w in (w_ffn_gate[l], w_ffn_up[l], w_ffn_down[l]))
        nw = norm_mix_pre[l]

        def finish(x, o_ret, o_diff, gates, tm):
            x = merge(x, o_ret, o_diff, gates, ret_norm[l], diff_norm[l], w_o, norm_mix_post[l], lam_init, tm)
            return ffn(x, norm_ffn_pre[l], w_g, w_u, w_d, norm_ffn_post[l], tm, 2)

        xm = meta_tokens.astype(F32)
        (ret_m,) = norm_matmul(xm, nw, w_ret, (BF16,), n_meta, 1024)
        (kv32_m,) = norm_matmul(xm, nw, w_kv, (F32,), n_meta, 1024)
        zero_state = jnp.zeros((1, RET_HEADS, RET_DK, d // RET_HEADS), F32)
        _, s_meta = retention(log_g, ret_m[None], zero_state, n_meta, n_meta)
        kv16_m = jnp.pad(kv32_m.astype(BF16), ((0, meta_pad - n_meta), (0, 0)))
        k_meta, vt_meta = kv16_m[:, :kv_w], kv16_m[:, kv_w:].T

        (ret_p,) = norm_matmul(xp, nw, w_ret, (BF16,), 1024, w_ret.shape[1])
        (gates_p,) = norm_matmul(xp, nw, w_gate, (BF16,), 1024, w_gate.shape[1])
        (q_p,) = norm_matmul(xp, nw, w_q, (BF16,), 1024, 1024)
        k32_p, k16_p = norm_matmul(xp, nw, w_k, (F32, BF16), 1024, 1024)
        v32_p, vt_p = value_proj(xp, nw, w_v, min(KV_CHUNK, seq), min(KV_CHUNK, seq))
        o_ret, s_fin = retention(log_g, ret_p.reshape(b, seq, -1), s_meta, 128, 128)
        o_diff = prompt_attention(slopes, lam_params, q_p.reshape(b, seq, kv_w), k16_p.reshape(b, seq, kv_w),
                                  vt_p, k_meta, vt_meta, lam_init)
        xp = finish(xp, o_ret.reshape(b * seq, d), o_diff.reshape(b * seq, d), gates_p, 512)
        for o, main, lo in ((outs[0], k32_p, 0), (outs[1], v32_p, kv_w)):
            meta_part = jnp.broadcast_to(kv32_m[None, :, lo:lo + kv_w], (b, n_meta, kv_w))
            both = jnp.concatenate([meta_part, main.reshape(b, seq, kv_w)], axis=1)
            o.append(both.reshape(b, n_meta + seq, DIFF_HEADS, HEAD_W))
        outs[2].append(s_fin)

        (ret_s,) = norm_matmul(xs, nw, w_ret, (BF16,), 512, 1024)
        (gates_s,) = norm_matmul(xs, nw, w_gate, (BF16,), 512, 1024)
        (q_s,) = norm_matmul(xs, nw, w_q, (F32,), 512, 1024)
        (kv32_s,) = norm_matmul(xs, nw, w_kv, (F32,), 512, 1024)
        ret_s = jnp.pad(ret_s.reshape(db, t_new, -1), ((0, 0), (0, pad_rows - t_new), (0, 0)))
        o_ret_s, s_new = retention(log_g, ret_s, state_ret[l], pad_rows, t_new)
        q_s = q_s.reshape(db, t_new, DIFF_HEADS, 1, HEAD_W).transpose(0, 2, 3, 1, 4)
        q_rows = jnp.broadcast_to(q_s, (db, DIFF_HEADS, 2, t_new, HEAD_W)).reshape(db, -1, HEAD_W)
        kv32_s = kv32_s.reshape(db, t_new, 2 * kv_w)
        to_rows = lambda a: jnp.pad(a.reshape(db, t_new * DIFF_HEADS, HEAD_W),
                                    ((0, 0), (0, new_pad - t_new * DIFF_HEADS), (0, 0)))
        n_rows = decode_attention(page_table, lam_params, slope_rows, q_rows, cache_k, cache_v, l,
                                  to_rows(kv32_s[:, :, :kv_w]), to_rows(kv32_s[:, :, kv_w:]), lam_init, t_new)
        o_diff_s = n_rows.reshape(db, DIFF_HEADS, 2, t_new, HEAD_W)[:, :, 0].transpose(0, 2, 1, 3)
        xs = finish(xs, o_ret_s[:, :t_new].reshape(db * t_new, d), o_diff_s.reshape(db * t_new, d), gates_s, 512)
        outs[3].append(kv32_s[:, :, :kv_w].reshape(db, t_new, DIFF_HEADS, HEAD_W))
        outs[4].append(kv32_s[:, :, kv_w:].reshape(db, t_new, DIFF_HEADS, HEAD_W))
        outs[5].append(s_new)

    kp, vp, sp, ks, vs, ss = (jnp.stack(o) for o in outs)
    return (xp.reshape(b, seq, d), xs.reshape(db, t_new, d), kp, vp, sp, ks, vs, ss)
```
